```python
import math
import jax, jax.numpy as jnp
from jax import lax
import numpy as np

D_MODEL = 4096
BATCH = 4
SEQ = 4096
DEPTH = 4
DEC_BATCH = 8
DEC_SEQ = 64
PAST_LEN = 4096

CHUNK = 64
EPS = 1e-6
POOL_WINDOWS = (2, 4, 8, 16)
N_POOL_GROUPS = 4
POOL_WIDTH = 3 * D_MODEL // 8
POOL_GROUP = POOL_WIDTH // N_POOL_GROUPS
POOL_HIST = 15
CONV_WIDTH = 3 * D_MODEL // 8
CONV_K = 31
CONV_HIST = CONV_K - 1
ATTN_WIDTH = D_MODEL // 4
N_MEM_HEADS = 4
MEM_HEAD_DIM = ATTN_WIDTH // N_MEM_HEADS
N_MEM = 256
N_BRANCH = 3
N_IN = 2 * POOL_WIDTH + 3 * CONV_WIDTH + 2 * ATTN_WIDTH + N_BRANCH * D_MODEL

kernel_name = "gated_pool_conv_memory_stream_encoder"


def _split_points():
    sizes = [POOL_WIDTH, POOL_WIDTH, CONV_WIDTH, CONV_WIDTH, CONV_WIDTH,
             ATTN_WIDTH, ATTN_WIDTH, D_MODEL, D_MODEL]
    pts, acc = [], 0
    for s in sizes:
        acc += s
        pts.append(acc)
    return pts


def _rms_norm(x, g):
    xf = x.astype(jnp.float32)
    y = xf * lax.rsqrt(jnp.mean(xf * xf, axis=-1, keepdims=True) + EPS)
    return (y * g.astype(jnp.float32)).astype(x.dtype)


def _pool_mix(v, hist, pos0, pool_w, pool_scale):
    B, L, _ = v.shape
    xx = jnp.concatenate([hist.astype(v.dtype), v], axis=1)
    cs = jnp.cumsum(xx.astype(jnp.float32), axis=1)
    cs = jnp.concatenate([jnp.zeros((B, 1, POOL_WIDTH), jnp.float32), cs], axis=1)
    cs = cs.reshape(B, L + POOL_HIST + 1, N_POOL_GROUPS, POOL_GROUP)
    pos = pos0 + jnp.arange(L)
    means = []
    for gi, w in enumerate(POOL_WINDOWS):
        s = (cs[:, POOL_HIST + 1:POOL_HIST + 1 + L, gi]
             - cs[:, POOL_HIST + 1 - w:POOL_HIST + 1 - w + L, gi])
        cnt = jnp.minimum(w, pos + 1).astype(jnp.float32)
        means.append(s / cnt[None, :, None])
    mean = jnp.stack(means, axis=2)
    d = (mean - v.reshape(B, L, N_POOL_GROUPS, POOL_GROUP).astype(jnp.float32)).astype(v.dtype)
    y = jnp.einsum('blgc,gcd->blgd', d, pool_w).reshape(B, L, POOL_WIDTH) * pool_scale
    return y, xx[:, -POOL_HIST:]


def _conv_module(a, b, hist, conv_w, conv_b, ln_g, ln_b):
    glu = a * jax.nn.sigmoid(b)
    xx = jnp.concatenate([hist.astype(glu.dtype), glu], axis=1)
    y = lax.conv_general_dilated(xx, conv_w[:, None, :], window_strides=(1,), padding='VALID',
                                 dimension_numbers=('NWC', 'WIO', 'NWC'),
                                 feature_group_count=CONV_WIDTH) + conv_b
    yf = y.astype(jnp.float32)
    mu = jnp.mean(yf, axis=-1, keepdims=True)
    var = jnp.mean(jnp.square(yf - mu), axis=-1, keepdims=True)
    yn = ((yf - mu) * lax.rsqrt(var + EPS) * ln_g.astype(jnp.float32)
          + ln_b.astype(jnp.float32)).astype(glu.dtype)
    return jax.nn.silu(yn), xx[:, -CONV_HIST:]


def _mem_kv(mem, mem_norm_g, w_k, w_v):
    B = mem.shape[0]
    hm = _rms_norm(mem, mem_norm_g)
    k = (hm @ w_k).reshape(B, N_MEM, N_MEM_HEADS, MEM_HEAD_DIM)
    v = (hm @ w_v).reshape(B, N_MEM, N_MEM_HEADS, MEM_HEAD_DIM)
    return k, v


def _mem_attend(q, k, v):
    B, L, _ = q.shape
    qh = q.reshape(B, L, N_MEM_HEADS, MEM_HEAD_DIM)
    s = jnp.einsum('blhd,bmhd->bhlm', qh, k).astype(jnp.float32) / math.sqrt(MEM_HEAD_DIM)
    p = jax.nn.softmax(s, axis=-1).astype(v.dtype)
    o = jnp.einsum('bhlm,bmhd->blhd', p, v)
    return o.reshape(B, L, ATTN_WIDTH)


def _layer(x, pool_hist, conv_hist, mk, mv, pos0, norm_g, w_in, pool_w, pool_scale,
           conv_w, conv_b, conv_ln_g, conv_ln_b, w_pool_out, w_conv_out, w_attn_out, w_out):
    h = _rms_norm(x, norm_g)
    u = h @ w_in
    pv, pz, ca, cb, cz, q, az, gp, gc, ga = jnp.split(u, _split_points(), axis=-1)
    yp, new_pool = _pool_mix(pv, pool_hist, pos0, pool_w, pool_scale)
    yp = (yp * jax.nn.silu(pz)) @ w_pool_out
    yc, new_conv = _conv_module(ca, cb, conv_hist, conv_w, conv_b, conv_ln_g, conv_ln_b)
    yc = (yc * jax.nn.silu(cz)) @ w_conv_out
    ya = _mem_attend(q, mk, mv)
    ya = (ya * jax.nn.silu(az)) @ w_attn_out
    merged = jax.nn.sigmoid(gp) * yp + jax.nn.sigmoid(gc) * yc + jax.nn.sigmoid(ga) * ya
    return x + merged @ w_out, new_pool, new_conv


def setup_inputs(seed: int = 0) -> dict:
    key = jax.random.key(seed)
    ks = jax.random.split(key, 24)
    f32 = jnp.float32
    n = lambda k, shape, s: jax.random.normal(k, shape, f32) * s
    return {
        "x_prompt": n(ks[0], (BATCH, SEQ, D_MODEL), 1.0),
        "x_sample": n(ks[1], (DEC_BATCH, DEC_SEQ, D_MODEL), 1.0),
        "cache_pool": n(ks[2], (DEPTH, DEC_BATCH, POOL_HIST, POOL_WIDTH), 1.0),
        "cache_conv": n(ks[3], (DEPTH, DEC_BATCH, CONV_HIST, CONV_WIDTH), 0.5),
        "cache_mem_k": n(ks[4], (DEPTH, DEC_BATCH, N_MEM, N_MEM_HEADS, MEM_HEAD_DIM), 1.0),
        "cache_mem_v": n(ks[5], (DEPTH, DEC_BATCH, N_MEM, N_MEM_HEADS, MEM_HEAD_DIM), 1.0),
        "mem_prompt": n(ks[6], (BATCH, N_MEM, D_MODEL), 1.0),
        "norm_g": 1.0 + n(ks[7], (DEPTH, D_MODEL), 0.02),
        "w_in": n(ks[8], (DEPTH, D_MODEL, N_IN), D_MODEL ** -0.5),
        "pool_w": n(ks[9], (DEPTH, N_POOL_GROUPS, POOL_GROUP, POOL_GROUP), POOL_GROUP ** -0.5),
        "pool_scale": 0.5 + n(ks[10], (DEPTH, POOL_WIDTH), 0.05),
        "conv_w": n(ks[11], (DEPTH, CONV_K, CONV_WIDTH), CONV_K ** -0.5),
        "conv_b": n(ks[12], (DEPTH, CONV_WIDTH), 0.02),
        "conv_ln_g": 1.0 + n(ks[13], (DEPTH, CONV_WIDTH), 0.02),
        "conv_ln_b": n(ks[14], (DEPTH, CONV_WIDTH), 0.02),
        "mem_norm_g": 1.0 + n(ks[15], (DEPTH, D_MODEL), 0.02),
        "w_k": n(ks[16], (DEPTH, D_MODEL, ATTN_WIDTH), D_MODEL ** -0.5),
        "w_v": n(ks[17], (DEPTH, D_MODEL, ATTN_WIDTH), D_MODEL ** -0.5),
        "w_pool_out": n(ks[18], (DEPTH, POOL_WIDTH, D_MODEL), POOL_WIDTH ** -0.5),
        "w_conv_out": n(ks[19], (DEPTH, CONV_WIDTH, D_MODEL), CONV_WIDTH ** -0.5),
        "w_attn_out": n(ks[20], (DEPTH, ATTN_WIDTH, D_MODEL), ATTN_WIDTH ** -0.5),
        "w_out": n(ks[21], (DEPTH, D_MODEL, D_MODEL), D_MODEL ** -0.5),
        "final_g": 1.0 + n(ks[22], (D_MODEL,), 0.02),
    }


def reference(x_prompt, x_sample, cache_pool, cache_conv, cache_mem_k, cache_mem_v, mem_prompt,
              norm_g, w_in, pool_w, pool_scale, conv_w, conv_b, conv_ln_g, conv_ln_b,
              mem_norm_g, w_k, w_v, w_pool_out, w_conv_out, w_attn_out, w_out, final_g):
    assert x_sample.shape[1] <= CHUNK
    Bp = x_prompt.shape[0]
    xp, xs = x_prompt, x_sample
    pool_p, conv_p, mk_p, mv_p, pool_s, conv_s = [], [], [], [], [], []
    zero_pool = jnp.zeros((Bp, POOL_HIST, POOL_WIDTH), x_prompt.dtype)
    zero_conv = jnp.zeros((Bp, CONV_HIST, CONV_WIDTH), x_prompt.dtype)
    for l in range(DEPTH):
        lw = (norm_g[l], w_in[l], pool_w[l], pool_scale[l], conv_w[l], conv_b[l],
              conv_ln_g[l], conv_ln_b[l], w_pool_out[l], w_conv_out[l], w_attn_out[l], w_out[l])
        mk, mv = _mem_kv(mem_prompt, mem_norm_g[l], w_k[l], w_v[l])
        xp, np_, nc_ = _layer(xp, zero_pool, zero_conv, mk, mv, 0, *lw)
        pool_p.append(np_); conv_p.append(nc_); mk_p.append(mk); mv_p.append(mv)
        xs, ns_, ncs_ = _layer(xs, cache_pool[l], cache_conv[l], cache_mem_k[l], cache_mem_v[l],
                               PAST_LEN, *lw)
        pool_s.append(ns_); conv_s.append(ncs_)
    y_prompt = _rms_norm(xp, final_g)
    y_sample = _rms_norm(xs, final_g)
    return (y_prompt, y_sample, jnp.stack(pool_p), jnp.stack(conv_p), jnp.stack(mk_p),
            jnp.stack(mv_p), jnp.stack(pool_s), jnp.stack(conv_s))
```

```python
import functools

import jax
import jax.numpy as jnp
from jax import lax
from jax.experimental import pallas as pl
from jax.experimental.pallas import tpu as pltpu

F32 = jnp.float32
BF16 = jnp.bfloat16

D_MODEL = 4096
EPS = 1e-6
POOL_WINDOWS = (2, 4, 8, 16)
POOL_WIDTH = 1536
POOL_GROUP = 384
POOL_HIST = 15
POOL_HIST_PAD = 16
CONV_WIDTH = 1536
CONV_K = 31
CONV_HIST = 30
CONV_HIST_PAD = 32
SUBLANES = 8
ROW_CHUNK = 16
ATTN_WIDTH = 1024
N_HEADS = 4
HEAD_DIM = 256
N_MEM = 256
N_IN = 2 * POOL_WIDTH + 3 * CONV_WIDTH + 2 * ATTN_WIDTH + 3 * D_MODEL

COL_BLK = 512
OFF_PV, OFF_PZ, OFF_CA, OFF_CB, OFF_CZ, OFF_Q, OFF_AZ, OFF_GP, OFF_GC, OFF_GA = (
    0, 3, 6, 9, 12, 15, 17, 19, 27, 35)

VMEM_LIMIT_BYTES = 56 * 1024 * 1024


def _params(n_axes):
    return pltpu.CompilerParams(
        dimension_semantics=("arbitrary",) * n_axes,
        vmem_limit_bytes=VMEM_LIMIT_BYTES)


def _sigmoid(x):
    return 1.0 / (1.0 + jnp.exp(-x))


def _silu(x):
    return x * _sigmoid(x)


def _rmsnorm_kernel(x_ref, g_ref, o_ref):
    x = x_ref[...]
    ms = jnp.mean(x * x, axis=-1, keepdims=True)
    o_ref[...] = (x * lax.rsqrt(ms + EPS) * g_ref[...]).astype(o_ref.dtype)


def _rmsnorm(x, g, out_dtype, name):
    m, d = x.shape
    tr = min(m, 256)
    return pl.pallas_call(
        _rmsnorm_kernel,
        out_shape=jax.ShapeDtypeStruct((m, d), out_dtype),
        grid=(m // tr,),
        in_specs=[pl.BlockSpec((tr, d), lambda i: (i, 0)),
                  pl.BlockSpec((1, d), lambda i: (0, 0))],
        out_specs=pl.BlockSpec((tr, d), lambda i: (i, 0)),
        compiler_params=_params(1),
        name=name,
    )(x, g.reshape(1, d))


def _mm_kernel(a_ref, w_ref, o_ref):
    o_ref[...] = jnp.dot(a_ref[...], w_ref[...],
                         preferred_element_type=F32).astype(o_ref.dtype)


def _mm_res_kernel(a_ref, w_ref, r_ref, o_ref):
    o_ref[...] = r_ref[...] + jnp.dot(a_ref[...], w_ref[...],
                                      preferred_element_type=F32)


def _matmul(a, w, *, tn, name, residual=None):
    m, k = a.shape
    n = w.shape[1]
    tm = min(m, 1024)
    in_specs = [pl.BlockSpec((tm, k), lambda i, j: (i, 0)),
                pl.BlockSpec((k, tn), lambda i, j: (0, j))]
    args = [a, w]
    kern = _mm_kernel
    if residual is not None:
        in_specs.append(pl.BlockSpec((tm, tn), lambda i, j: (i, j)))
        args.append(residual)
        kern = _mm_res_kernel
    return pl.pallas_call(
        kern,
        out_shape=jax.ShapeDtypeStruct((m, n), F32),
        grid=(m // tm, n // tn),
        in_specs=in_specs,
        out_specs=pl.BlockSpec((tm, tn), lambda i, j: (i, j)),
        compiler_params=_params(2),
        name=name,
    )(*args)


def _pool_kernel(pv_ref, prev_ref, hist_ref, pz_ref, pw_ref, ps_ref,
                 y_ref, newhist_ref, xx_ref, d_ref, *, tl, rc, pos0):
    j = pl.program_id(1)

    @pl.when(j == 0)
    def _():
        xx_ref[0:POOL_HIST_PAD, :] = hist_ref[0]

    @pl.when(j > 0)
    def _():
        xx_ref[0:POOL_HIST_PAD, :] = prev_ref[...]

    xx_ref[POOL_HIST_PAD:POOL_HIST_PAD + tl, :] = pv_ref[...]

    for g, w in enumerate(POOL_WINDOWS):
        cols = slice(g * POOL_GROUP, (g + 1) * POOL_GROUP)

        for r0 in range(0, tl, rc):
            base = POOL_HIST_PAD + r0
            v = xx_ref[base:base + rc, cols]
            acc = v
            for i in range(1, w):
                acc = acc + xx_ref[base - i:base - i + rc, cols]
            t = j * tl + r0 + lax.broadcasted_iota(jnp.int32, (rc, POOL_GROUP), 0)
            cnt = jnp.minimum(w, pos0 + t + 1).astype(F32)
            d_ref[r0:r0 + rc, cols] = (acc / cnt - v).astype(BF16)

    for g in range(len(POOL_WINDOWS)):
        cols = slice(g * POOL_GROUP, (g + 1) * POOL_GROUP)
        yg = jnp.dot(d_ref[:, cols], pw_ref[g], preferred_element_type=F32)
        yg = yg * ps_ref[:, cols] * _silu(pz_ref[:, cols])
        y_ref[:, cols] = yg.astype(BF16)

    @pl.when(j == pl.num_programs(1) - 1)
    def _():
        newhist_ref[0] = xx_ref[POOL_HIST_PAD + tl - POOL_HIST:POOL_HIST_PAD + tl, :]


def _pool_branch(u, hist_pad, pw, ps, *, nb, sl, pos0):
    tl = min(sl, 512)
    nj = sl // tl
    rc = min(tl, 64)
    kern = functools.partial(_pool_kernel, tl=tl, rc=rc, pos0=pos0)
    prev_blocks = tl // POOL_HIST_PAD
    return pl.pallas_call(
        kern,
        out_shape=(jax.ShapeDtypeStruct((nb * sl, POOL_WIDTH), BF16),
                   jax.ShapeDtypeStruct((nb, POOL_HIST, POOL_WIDTH), F32)),
        grid=(nb, nj),
        in_specs=[
            pl.BlockSpec((tl, POOL_WIDTH), lambda b, j: (b * nj + j, 0)),
            pl.BlockSpec((POOL_HIST_PAD, POOL_WIDTH),
                         lambda b, j: (jnp.maximum((b * nj + j) * prev_blocks - 1, 0), 0)),
            pl.BlockSpec((1, POOL_HIST_PAD, POOL_WIDTH), lambda b, j: (b, 0, 0)),
            pl.BlockSpec((tl, POOL_WIDTH), lambda b, j: (b * nj + j, 1)),
            pl.BlockSpec((len(POOL_WINDOWS), POOL_GROUP, POOL_GROUP), lambda b, j: (0, 0, 0)),
            pl.BlockSpec((1, POOL_WIDTH), lambda b, j: (0, 0)),
        ],
        out_specs=(pl.BlockSpec((tl, POOL_WIDTH), lambda b, j: (b * nj + j, 0)),
                   pl.BlockSpec((1, POOL_HIST, POOL_WIDTH), lambda b, j: (b, 0, 0))),
        scratch_shapes=[pltpu.VMEM((POOL_HIST_PAD + tl, POOL_WIDTH), F32),
                        pltpu.VMEM((tl, POOL_WIDTH), BF16)],
        compiler_params=_params(2),
        name="pool_branch",
    )(u, u, hist_pad, u, pw, ps)


def _conv_kernel(ca_ref, cb_ref, cz_ref, pa_ref, pb_ref, hist_ref, w_ref, b_ref,
                 g_ref, beta_ref, y_ref, newhist_ref, gl_ref, yc_ref, *, tl, lw):
    j = pl.program_id(1)

    @pl.when(j == 0)
    def _():
        gl_ref[0:CONV_HIST_PAD, :] = hist_ref[0]

    @pl.when(j > 0)
    def _():
        gl_ref[0:CONV_HIST_PAD, :] = pa_ref[...] * _sigmoid(pb_ref[...])

    def glu_chunk(c, carry):
        r = pl.multiple_of(c * ROW_CHUNK, ROW_CHUNK)
        gl_ref[pl.ds(CONV_HIST_PAD + r, ROW_CHUNK), :] = (
            ca_ref[pl.ds(r, ROW_CHUNK), :] * _sigmoid(cb_ref[pl.ds(r, ROW_CHUNK), :]))
        return carry

    lax.fori_loop(0, tl // ROW_CHUNK, glu_chunk, 0)

    first_tap = CONV_HIST_PAD - CONV_HIST
    row = lax.broadcasted_iota(jnp.int32, (SUBLANES, lw), 0)
    n_tiles = tl // SUBLANES
    for lb in range(CONV_WIDTH // lw):
        lanes = slice(lb * lw, (lb + 1) * lw)

        def wrow(o, lanes=lanes):
            k = o - first_tap
            return w_ref[k:k + 1, lanes]

        def phase_sums(t0, lanes=lanes, wrow=wrow):
            tiles = [gl_ref[pl.ds(pl.multiple_of((t0 + q) * SUBLANES, SUBLANES), SUBLANES), lanes]
                     for q in range(4)]
            zs = []
            for r in range(1, SUBLANES):
                acc = None
                for q in range(4):
                    o = SUBLANES * q + r
                    if o < first_tap:
                        continue
                    term = tiles[q] * wrow(o)
                    acc = term if acc is None else acc + term
                zs.append(acc)
            return tiles, tuple(zs)

        def body(i, zcur, lanes=lanes, wrow=wrow, phase_sums=phase_sums):
            tiles, znext = phase_sums(i + 1)
            acc = b_ref[:, lanes] + tiles[0] * wrow(SUBLANES)
            for q in range(1, 4):
                acc = acc + tiles[q] * wrow(SUBLANES * (q + 1))
            for r in range(1, SUBLANES):
                m = jnp.where(row >= r, zcur[r - 1], znext[r - 1])
                acc = acc + pltpu.roll(m, SUBLANES - r, axis=0)
            yc_ref[pl.ds(pl.multiple_of(i * SUBLANES, SUBLANES), SUBLANES), lanes] = acc
            return znext

        lax.fori_loop(0, n_tiles, body, phase_sums(0)[1])

    def ln_chunk(c, carry):
        r = pl.multiple_of(c * ROW_CHUNK, ROW_CHUNK)
        y = yc_ref[pl.ds(r, ROW_CHUNK), :]
        mu = jnp.mean(y, axis=-1, keepdims=True)
        xc = y - mu
        var = jnp.mean(xc * xc, axis=-1, keepdims=True)
        yn = xc * lax.rsqrt(var + EPS) * g_ref[...] + beta_ref[...]
        out = _silu(yn) * _silu(cz_ref[pl.ds(r, ROW_CHUNK), :])
        y_ref[pl.ds(r, ROW_CHUNK), :] = out.astype(BF16)
        return carry

    lax.fori_loop(0, tl // ROW_CHUNK, ln_chunk, 0)

    @pl.when(j == pl.num_programs(1) - 1)
    def _():
        newhist_ref[0] = gl_ref[CONV_HIST_PAD + tl - CONV_HIST:CONV_HIST_PAD + tl, :]


def _conv_branch(u, hist_pad, cw, cb, lg, lb, *, nb, sl):
    tl = min(sl, 512)
    nj = sl // tl
    kern = functools.partial(_conv_kernel, tl=tl, lw=256)
    prev_blocks = tl // CONV_HIST_PAD
    row = lambda b, j: b * nj + j
    prev = lambda b, j: jnp.maximum((b * nj + j) * prev_blocks - 1, 0)
    vec = pl.BlockSpec((1, CONV_WIDTH), lambda b, j: (0, 0))
    return pl.pallas_call(
        kern,
        out_shape=(jax.ShapeDtypeStruct((nb * sl, CONV_WIDTH), BF16),
                   jax.ShapeDtypeStruct((nb, CONV_HIST, CONV_WIDTH), F32)),
        grid=(nb, nj),
        in_specs=[
            pl.BlockSpec((tl, CONV_WIDTH), lambda b, j: (row(b, j), 2)),
            pl.BlockSpec((tl, CONV_WIDTH), lambda b, j: (row(b, j), 3)),
            pl.BlockSpec((tl, CONV_WIDTH), lambda b, j: (row(b, j), 4)),
            pl.BlockSpec((CONV_HIST_PAD, CONV_WIDTH), lambda b, j: (prev(b, j), 2)),
            pl.BlockSpec((CONV_HIST_PAD, CONV_WIDTH), lambda b, j: (prev(b, j), 3)),
            pl.BlockSpec((1, CONV_HIST_PAD, CONV_WIDTH), lambda b, j: (b, 0, 0)),
            pl.BlockSpec((CONV_HIST_PAD, CONV_WIDTH), lambda b, j: (0, 0)),
            vec, vec, vec,
        ],
        out_specs=(pl.BlockSpec((tl, CONV_WIDTH), lambda b, j: (row(b, j), 0)),
                   pl.BlockSpec((1, CONV_HIST, CONV_WIDTH), lambda b, j: (b, 0, 0))),
        scratch_shapes=[pltpu.VMEM((CONV_HIST_PAD + tl, CONV_WIDTH), F32),
                        pltpu.VMEM((tl, CONV_WIDTH), F32)],
        compiler_params=_params(2),
        name="conv_branch",
    )(u, u, u, u, u, hist_pad, cw, cb, lg, lb)


def _attn_kernel(q_ref, az_ref, k_ref, v_ref, y_ref):
    q = q_ref[...].astype(BF16)
    k = k_ref[0].astype(BF16)
    s = lax.dot_general(q, k, (((1,), (1,)), ((), ())), preferred_element_type=F32)
    s = s * (1.0 / 16.0)
    m = jnp.max(s, axis=-1, keepdims=True)
    e = jnp.exp(s - m)
    p = e / jnp.sum(e, axis=-1, keepdims=True)
    o = jnp.dot(p.astype(BF16), v_ref[0].astype(BF16), preferred_element_type=F32)
    y_ref[...] = (o * _silu(az_ref[...])).astype(BF16)


def _attn_branch(u, mk, mv, *, nb, sl):
    tl = min(sl, 512)
    nj = sl // tl
    qb = COL_BLK // HEAD_DIM
    return pl.pallas_call(
        _attn_kernel,
        out_shape=jax.ShapeDtypeStruct((nb * sl, ATTN_WIDTH), BF16),
        grid=(nb, nj, N_HEADS),
        in_specs=[
            pl.BlockSpec((tl, HEAD_DIM), lambda b, j, h: (b * nj + j, OFF_Q * qb + h)),
            pl.BlockSpec((tl, HEAD_DIM), lambda b, j, h: (b * nj + j, OFF_AZ * qb + h)),
            pl.BlockSpec((1, N_MEM, HEAD_DIM), lambda b, j, h: (b, 0, h)),
            pl.BlockSpec((1, N_MEM, HEAD_DIM), lambda b, j, h: (b, 0, h)),
        ],
        out_specs=pl.BlockSpec((tl, HEAD_DIM), lambda b, j, h: (b * nj + j, h)),
        compiler_params=_params(3),
        name="attn_branch",
    )(u, u, mk, mv)


def _merge_kernel(yp_ref, yc_ref, ya_ref, wp_ref, wc_ref, wa_ref,
                  gp_ref, gc_ref, ga_ref, o_ref):
    yp = jnp.dot(yp_ref[...], wp_ref[...], preferred_element_type=F32)
    yc = jnp.dot(yc_ref[...], wc_ref[...], preferred_element_type=F32)
    ya = jnp.dot(ya_ref[...], wa_ref[...], preferred_element_type=F32)
    merged = (_sigmoid(gp_ref[...]) * yp + _sigmoid(gc_ref[...]) * yc
              + _sigmoid(ga_ref[...]) * ya)
    o_ref[...] = merged.astype(BF16)


def _merge(u, yp, yc, ya, wp, wc, wa):
    m = u.shape[0]
    tm = min(m, 1024)
    tn = COL_BLK
    act = lambda width: pl.BlockSpec((tm, width), lambda i, j: (i, 0))
    wgt = lambda width: pl.BlockSpec((width, tn), lambda i, j: (0, j))
    gate = lambda off: pl.BlockSpec((tm, tn), lambda i, j: (i, off + j))
    return pl.pallas_call(
        _merge_kernel,
        out_shape=jax.ShapeDtypeStruct((m, D_MODEL), BF16),
        grid=(m // tm, D_MODEL // tn),
        in_specs=[act(POOL_WIDTH), act(CONV_WIDTH), act(ATTN_WIDTH),
                  wgt(POOL_WIDTH), wgt(CONV_WIDTH), wgt(ATTN_WIDTH),
                  gate(OFF_GP), gate(OFF_GC), gate(OFF_GA)],
        out_specs=pl.BlockSpec((tm, tn), lambda i, j: (i, j)),
        compiler_params=_params(2),
        name="merge",
    )(yp, yc, ya, wp, wc, wa, u, u, u)


def _layer(x, pool_hist_pad, conv_hist_pad, mk, mv, lw, *, nb, sl, pos0):
    h = _rmsnorm(x, lw["norm_g"], BF16, "rmsnorm")
    u = _matmul(h, lw["w_in"], tn=COL_BLK, name="in_proj")
    yp, new_pool = _pool_branch(u, pool_hist_pad, lw["pool_w"], lw["pool_scale"],
                                nb=nb, sl=sl, pos0=pos0)
    yc, new_conv = _conv_branch(u, conv_hist_pad, lw["conv_w"], lw["conv_b"],
                                lw["conv_ln_g"], lw["conv_ln_b"], nb=nb, sl=sl)
    ya = _attn_branch(u, mk, mv, nb=nb, sl=sl)
    merged = _merge(u, yp, yc, ya, lw["w_pool_out"], lw["w_conv_out"], lw["w_attn_out"])
    x_new = _matmul(merged, lw["w_out"], tn=COL_BLK, name="out_proj", residual=x)
    return x_new, new_pool, new_conv


def kernel(x_prompt, x_sample, cache_pool, cache_conv, cache_mem_k, cache_mem_v, mem_prompt,
           norm_g, w_in, pool_w, pool_scale, conv_w, conv_b, conv_ln_g, conv_ln_b,
           mem_norm_g, w_k, w_v, w_pool_out, w_conv_out, w_attn_out, w_out, final_g):
    bp, sp, d = x_prompt.shape
    bs, ss, _ = x_sample.shape
    depth = w_in.shape[0]
    past_len = 4096

    xp = x_prompt.reshape(bp * sp, d)
    xs = x_sample.reshape(bs * ss, d)
    mem = mem_prompt.reshape(bp * N_MEM, d)

    zero_pool = jnp.zeros((bp, POOL_HIST_PAD, POOL_WIDTH), F32)
    zero_conv = jnp.zeros((bp, CONV_HIST_PAD, CONV_WIDTH), F32)
    pool_pad = jnp.pad(cache_pool, ((0, 0), (0, 0), (POOL_HIST_PAD - POOL_HIST, 0), (0, 0)))
    conv_pad = jnp.pad(cache_conv, ((0, 0), (0, 0), (CONV_HIST_PAD - CONV_HIST, 0), (0, 0)))
    conv_w_pad = jnp.pad(conv_w, ((0, 0), (0, CONV_HIST_PAD - CONV_K), (0, 0)))

    pool_p, conv_p, mk_p, mv_p, pool_s, conv_s = [], [], [], [], [], []
    for l in range(depth):
        lw = {
            "norm_g": norm_g[l],
            "w_in": w_in[l].astype(BF16),
            "pool_w": pool_w[l].astype(BF16),
            "pool_scale": pool_scale[l].reshape(1, POOL_WIDTH),
            "conv_w": conv_w_pad[l],
            "conv_b": conv_b[l].reshape(1, CONV_WIDTH),
            "conv_ln_g": conv_ln_g[l].reshape(1, CONV_WIDTH),
            "conv_ln_b": conv_ln_b[l].reshape(1, CONV_WIDTH),
            "w_pool_out": w_pool_out[l].astype(BF16),
            "w_conv_out": w_conv_out[l].astype(BF16),
            "w_attn_out": w_attn_out[l].astype(BF16),
            "w_out": w_out[l].astype(BF16),
        }
        hm = _rmsnorm(mem, mem_norm_g[l], BF16, "mem_rmsnorm")
        mk = _matmul(hm, w_k[l].astype(BF16), tn=COL_BLK, name="mem_k")
        mv = _matmul(hm, w_v[l].astype(BF16), tn=COL_BLK, name="mem_v")
        mk = mk.reshape(bp, N_MEM, ATTN_WIDTH)
        mv = mv.reshape(bp, N_MEM, ATTN_WIDTH)
        xp, np_, nc_ = _layer(xp, zero_pool, zero_conv, mk, mv, lw, nb=bp, sl=sp, pos0=0)
        pool_p.append(np_)
        conv_p.append(nc_)
        mk_p.append(mk.reshape(bp, N_MEM, N_HEADS, HEAD_DIM))
        mv_p.append(mv.reshape(bp, N_MEM, N_HEADS, HEAD_DIM))

        ck = cache_mem_k[l].reshape(bs, N_MEM, ATTN_WIDTH)
        cv = cache_mem_v[l].reshape(bs, N_MEM, ATTN_WIDTH)
        xs, ns_, ncs_ = _layer(xs, pool_pad[l], conv_pad[l], ck, cv, lw,
                               nb=bs, sl=ss, pos0=past_len)
        pool_s.append(ns_)
        conv_s.append(ncs_)

    y_prompt = _rmsnorm(xp, final_g, F32, "final_rmsnorm").reshape(bp, sp, d)
    y_sample = _rmsnorm(xs, final_g, F32, "final_rmsnorm").reshape(bs, ss, d)
    return (y_prompt, y_sample, jnp.stack(pool_p), jnp.stack(conv_p), jnp.stack(mk_p),
            jnp.stack(mv_p), jnp.stack(pool_s), jnp.stack(conv_s))
```

```python
import functools

import jax
import jax.numpy as jnp
from jax import lax
from jax.experimental import pallas as pl
from jax.experimental.pallas import tpu as pltpu

F32 = jnp.float32
BF16 = jnp.bfloat16

D_MODEL = 4096
EPS = 1e-6
PAST_LEN = 4096
POOL_WINDOWS = (2, 4, 8, 16)
POOL_WIDTH = 1536
POOL_GROUP = 384
POOL_HIST = 15
POOL_HIST_PAD = 16
CONV_WIDTH = 1536
CONV_K = 31
CONV_HIST = 30
CONV_HIST_PAD = 32
SUBLANES = 8
ROW_CHUNK = 16
ATTN_WIDTH = 1024
N_HEADS = 4
HEAD_DIM = 256
N_MEM = 256
N_IN = 2 * POOL_WIDTH + 3 * CONV_WIDTH + 2 * ATTN_WIDTH + 3 * D_MODEL

COL_BLK = 512
OFF_PV, OFF_PZ, OFF_CA, OFF_CB, OFF_CZ, OFF_Q, OFF_AZ, OFF_GP, OFF_GC, OFF_GA = (
    0, 3, 6, 9, 12, 15, 17, 19, 27, 35)

VMEM_LIMIT_BYTES = 56 * 1024 * 1024


def _params(n_axes):
    return pltpu.CompilerParams(
        dimension_semantics=("arbitrary",) * n_axes,
        vmem_limit_bytes=VMEM_LIMIT_BYTES)


NEG_LOG2_E = -1.4426950408889634


def _sigmoid(x):
    return 1.0 / (1.0 + jnp.exp2(x * NEG_LOG2_E))


def _silu(x):
    return x * _sigmoid(x)


def _rmsnorm_kernel(x_ref, g_ref, o_ref):
    x = x_ref[...]
    ms = jnp.mean(x * x, axis=-1, keepdims=True)
    o_ref[...] = (x * lax.rsqrt(ms + EPS) * g_ref[...]).astype(o_ref.dtype)


def _rmsnorm(x, g, out_dtype, name):
    m, d = x.shape
    tr = min(m, 256)
    return pl.pallas_call(
        _rmsnorm_kernel,
        out_shape=jax.ShapeDtypeStruct((m, d), out_dtype),
        grid=(m // tr,),
        in_specs=[pl.BlockSpec((tr, d), lambda i: (i, 0)),
                  pl.BlockSpec((1, d), lambda i: (0, 0))],
        out_specs=pl.BlockSpec((tr, d), lambda i: (i, 0)),
        compiler_params=_params(1),
        name=name,
    )(x, g.reshape(1, d))


def _mm_kernel(a_ref, w_ref, o_ref):
    o_ref[...] = jnp.dot(a_ref[...], w_ref[...],
                         preferred_element_type=F32).astype(o_ref.dtype)


def _mm_res_kernel(a_ref, w_ref, r_ref, o_ref):
    o_ref[...] = r_ref[...] + jnp.dot(a_ref[...], w_ref[...],
                                      preferred_element_type=F32)


def _matmul(a, w, l, *, tm, tn, name, residual=None):
    m, k = a.shape
    n = w.shape[2]
    tm = min(m, tm)
    in_specs = [pl.BlockSpec((tm, k), lambda i, j: (i, 0)),
                pl.BlockSpec((None, k, tn), lambda i, j: (l, 0, j))]
    args = [a, w]
    kern = _mm_kernel
    if residual is not None:
        in_specs.append(pl.BlockSpec((tm, tn), lambda i, j: (i, j)))
        args.append(residual)
        kern = _mm_res_kernel
    return pl.pallas_call(
        kern,
        out_shape=jax.ShapeDtypeStruct((m, n), F32),
        grid=(m // tm, n // tn),
        in_specs=in_specs,
        out_specs=pl.BlockSpec((tm, tn), lambda i, j: (i, j)),
        compiler_params=_params(2),
        name=name,
    )(*args)


def _pool_kernel(pv_ref, prev_ref, hist_ref, pz_ref, pw_ref, ps_ref,
                 y_ref, newhist_ref, xx_ref, d_ref, *, tl, rc, pos0):
    j = pl.program_id(1)

    @pl.when(j == 0)
    def _():
        xx_ref[0:POOL_HIST_PAD, :] = hist_ref[0]

    @pl.when(j > 0)
    def _():
        xx_ref[0:POOL_HIST_PAD, :] = prev_ref[...]

    xx_ref[POOL_HIST_PAD:POOL_HIST_PAD + tl, :] = pv_ref[...]

    for g, w in enumerate(POOL_WINDOWS):
        cols = slice(g * POOL_GROUP, (g + 1) * POOL_GROUP)
        for r0 in range(0, tl, rc):
            base = POOL_HIST_PAD + r0
            v = xx_ref[base:base + rc, cols]
            acc = v
            for i in range(1, w):
                acc = acc + xx_ref[base - i:base - i + rc, cols]
            t = j * tl + r0 + lax.broadcasted_iota(jnp.int32, (rc, POOL_GROUP), 0)
            cnt = jnp.minimum(w, pos0 + t + 1).astype(F32)
            d_ref[r0:r0 + rc, cols] = (acc / cnt - v).astype(BF16)

    for g in range(len(POOL_WINDOWS)):
        cols = slice(g * POOL_GROUP, (g + 1) * POOL_GROUP)
        yg = jnp.dot(d_ref[:, cols], pw_ref[g], preferred_element_type=F32)
        yg = yg * ps_ref[:, cols] * _silu(pz_ref[:, cols])
        y_ref[:, cols] = yg.astype(BF16)

    @pl.when(j == pl.num_programs(1) - 1)
    def _():
        newhist_ref[0] = xx_ref[POOL_HIST_PAD + tl - POOL_HIST:POOL_HIST_PAD + tl, :]


def _pool_branch(u, hist_pad, hist_b0, pw, ps, l, *, nb, sl, pos0):
    tl = min(sl, 512)
    nj = sl // tl
    rc = min(tl, 64)
    kern = functools.partial(_pool_kernel, tl=tl, rc=rc, pos0=pos0)
    prev_blocks = tl // POOL_HIST_PAD
    n_groups = len(POOL_WINDOWS)
    return pl.pallas_call(
        kern,
        out_shape=(jax.ShapeDtypeStruct((nb * sl, POOL_WIDTH), BF16),
                   jax.ShapeDtypeStruct((nb, POOL_HIST, POOL_WIDTH), F32)),
        grid=(nb, nj),
        in_specs=[
            pl.BlockSpec((tl, POOL_WIDTH), lambda b, j: (b * nj + j, 0)),
            pl.BlockSpec((POOL_HIST_PAD, POOL_WIDTH),
                         lambda b, j: (jnp.maximum((b * nj + j) * prev_blocks - 1, 0), 0)),
            pl.BlockSpec((1, POOL_HIST_PAD, POOL_WIDTH), lambda b, j: (hist_b0 + b, 0, 0)),
            pl.BlockSpec((tl, POOL_WIDTH), lambda b, j: (b * nj + j, 1)),
            pl.BlockSpec((None, n_groups, POOL_GROUP, POOL_GROUP), lambda b, j: (l, 0, 0, 0)),
            pl.BlockSpec((None, 1, POOL_WIDTH), lambda b, j: (l, 0, 0)),
        ],
        out_specs=(pl.BlockSpec((tl, POOL_WIDTH), lambda b, j: (b * nj + j, 0)),
                   pl.BlockSpec((1, POOL_HIST, POOL_WIDTH), lambda b, j: (b, 0, 0))),
        scratch_shapes=[pltpu.VMEM((POOL_HIST_PAD + tl, POOL_WIDTH), F32),
                        pltpu.VMEM((tl, POOL_WIDTH), BF16)],
        compiler_params=_params(2),
        name="pool_branch",
    )(u, u, hist_pad, u, pw, ps)


def _conv_kernel(ca_ref, cb_ref, cz_ref, pa_ref, pb_ref, hist_ref, w_ref,
                 g_ref, beta_ref, y_ref, newhist_ref, gl_ref, yc_ref, *, tl, lw):
    j = pl.program_id(1)

    @pl.when(j == 0)
    def _():
        gl_ref[0:CONV_HIST_PAD, :] = hist_ref[0]

    @pl.when(j > 0)
    def _():
        gl_ref[0:CONV_HIST_PAD, :] = pa_ref[...] * _sigmoid(pb_ref[...])

    def glu_chunk(c, carry):
        r = pl.multiple_of(c * ROW_CHUNK, ROW_CHUNK)
        gl_ref[pl.ds(CONV_HIST_PAD + r, ROW_CHUNK), :] = (
            ca_ref[pl.ds(r, ROW_CHUNK), :] * _sigmoid(cb_ref[pl.ds(r, ROW_CHUNK), :]))
        return carry

    lax.fori_loop(0, tl // ROW_CHUNK, glu_chunk, 0)

    first_tap = CONV_HIST_PAD - CONV_HIST
    row = lax.broadcasted_iota(jnp.int32, (SUBLANES, lw), 0)
    n_tiles = tl // SUBLANES
    for lb in range(CONV_WIDTH // lw):
        lanes = slice(lb * lw, (lb + 1) * lw)

        def wrow(o, lanes=lanes):
            return w_ref[o - first_tap, :, lanes]

        def phase_sums(t0, lanes=lanes, wrow=wrow):
            tiles = [gl_ref[pl.ds(pl.multiple_of((t0 + q) * SUBLANES, SUBLANES), SUBLANES), lanes]
                     for q in range(4)]
            zs = []
            for r in range(1, SUBLANES):
                acc = None
                for q in range(4):
                    o = SUBLANES * q + r
                    if o < first_tap:
                        continue
                    term = tiles[q] * wrow(o)
                    acc = term if acc is None else acc + term
                zs.append(acc)
            return tiles, tuple(zs)

        def body(i, zcur, lanes=lanes, wrow=wrow, phase_sums=phase_sums):
            tiles, znext = phase_sums(i + 1)
            parts = [w_ref[CONV_K, :, lanes] + tiles[0] * wrow(SUBLANES)]
            for q in range(1, 4):
                parts.append(tiles[q] * wrow(SUBLANES * (q + 1)))
            for r in range(1, SUBLANES):
                m = jnp.where(row >= r, zcur[r - 1], znext[r - 1])
                parts.append(pltpu.roll(m, SUBLANES - r, axis=0))
            while len(parts) > 1:
                parts = [parts[p] + parts[p + 1] if p + 1 < len(parts) else parts[p]
                         for p in range(0, len(parts), 2)]
            yc_ref[pl.ds(pl.multiple_of(i * SUBLANES, SUBLANES), SUBLANES), lanes] = parts[0]
            return znext

        lax.fori_loop(0, n_tiles, body, phase_sums(0)[1], unroll=4)

    def ln_chunk(c, carry):
        r = pl.multiple_of(c * ROW_CHUNK, ROW_CHUNK)
        halves = []
        for half in range(ROW_CHUNK // SUBLANES):
            rows = pl.ds(pl.multiple_of(r + half * SUBLANES, SUBLANES), SUBLANES)
            y = yc_ref[rows, :]
            mu = jnp.mean(y, axis=-1, keepdims=True)
            xc = y - mu
            var = jnp.mean(xc * xc, axis=-1, keepdims=True)
            yn = xc * lax.rsqrt(var + EPS) * g_ref[...] + beta_ref[...]
            halves.append(_silu(yn) * _silu(cz_ref[rows, :]))
        y_ref[pl.ds(r, ROW_CHUNK), :] = jnp.concatenate(halves, axis=0).astype(BF16)
        return carry

    lax.fori_loop(0, tl // ROW_CHUNK, ln_chunk, 0, unroll=2)

    @pl.when(j == pl.num_programs(1) - 1)
    def _():
        newhist_ref[0] = gl_ref[CONV_HIST_PAD + tl - CONV_HIST:CONV_HIST_PAD + tl, :]


def _conv_branch(u, hist_pad, hist_b0, cw, lg, lb, l, *, nb, sl):
    tl = min(sl, 512)
    nj = sl // tl
    kern = functools.partial(_conv_kernel, tl=tl, lw=128)
    prev_blocks = tl // CONV_HIST_PAD
    row = lambda b, j: b * nj + j
    prev = lambda b, j: jnp.maximum((b * nj + j) * prev_blocks - 1, 0)
    vec = pl.BlockSpec((None, 1, CONV_WIDTH), lambda b, j: (l, 0, 0))
    return pl.pallas_call(
        kern,
        out_shape=(jax.ShapeDtypeStruct((nb * sl, CONV_WIDTH), BF16),
                   jax.ShapeDtypeStruct((nb, CONV_HIST, CONV_WIDTH), F32)),
        grid=(nb, nj),
        in_specs=[
            pl.BlockSpec((tl, CONV_WIDTH), lambda b, j: (row(b, j), 2)),
            pl.BlockSpec((tl, CONV_WIDTH), lambda b, j: (row(b, j), 3)),
            pl.BlockSpec((tl, CONV_WIDTH), lambda b, j: (row(b, j), 4)),
            pl.BlockSpec((CONV_HIST_PAD, CONV_WIDTH), lambda b, j: (prev(b, j), 2)),
            pl.BlockSpec((CONV_HIST_PAD, CONV_WIDTH), lambda b, j: (prev(b, j), 3)),
            pl.BlockSpec((1, CONV_HIST_PAD, CONV_WIDTH), lambda b, j: (hist_b0 + b, 0, 0)),
            pl.BlockSpec((None, CONV_K + 1, SUBLANES, CONV_WIDTH), lambda b, j: (l, 0, 0, 0)),
            vec, vec,
        ],
        out_specs=(pl.BlockSpec((tl, CONV_WIDTH), lambda b, j: (row(b, j), 0)),
                   pl.BlockSpec((1, CONV_HIST, CONV_WIDTH), lambda b, j: (b, 0, 0))),
        scratch_shapes=[pltpu.VMEM((CONV_HIST_PAD + tl, CONV_WIDTH), F32),
                        pltpu.VMEM((tl, CONV_WIDTH), F32)],
        compiler_params=_params(2),
        name="conv_branch",
    )(u, u, u, u, u, hist_pad, cw, lg, lb)


def _attn_kernel(q0_ref, q1_ref, z0_ref, z1_ref, k_ref, v_ref, y_ref):
    heads_per_blk = COL_BLK // HEAD_DIM
    for h in range(N_HEADS):
        q_ref = (q0_ref, q1_ref)[h // heads_per_blk]
        z_ref = (z0_ref, z1_ref)[h // heads_per_blk]
        c = slice((h % heads_per_blk) * HEAD_DIM, (h % heads_per_blk + 1) * HEAD_DIM)
        hc = slice(h * HEAD_DIM, (h + 1) * HEAD_DIM)
        q = q_ref[:, c].astype(BF16)
        k = k_ref[0, :, hc].astype(BF16)
        s = lax.dot_general(q, k, (((1,), (1,)), ((), ())), preferred_element_type=F32)
        s = s * (1.0 / 16.0)
        m = jnp.max(s, axis=-1, keepdims=True)
        e = jnp.exp(s - m)
        p = e / jnp.sum(e, axis=-1, keepdims=True)
        o = jnp.dot(p.astype(BF16), v_ref[0, :, hc].astype(BF16), preferred_element_type=F32)
        y_ref[:, hc] = (o * _silu(z_ref[:, c])).astype(BF16)


def _attn_branch(u, mk, mv, kv_b0, *, nb, sl):
    tl = min(sl, 512)
    nj = sl // tl
    ublk = lambda off: pl.BlockSpec((tl, COL_BLK), lambda b, j: (b * nj + j, off))
    kv = pl.BlockSpec((1, N_MEM, ATTN_WIDTH), lambda b, j: (kv_b0 + b, 0, 0))
    return pl.pallas_call(
        _attn_kernel,
        out_shape=jax.ShapeDtypeStruct((nb * sl, ATTN_WIDTH), BF16),
        grid=(nb, nj),
        in_specs=[ublk(OFF_Q), ublk(OFF_Q + 1), ublk(OFF_AZ), ublk(OFF_AZ + 1), kv, kv],
        out_specs=pl.BlockSpec((tl, ATTN_WIDTH), lambda b, j: (b * nj + j, 0)),
        compiler_params=_params(2),
        name="attn_branch",
    )(u, u, u, u, mk, mv)


def _merge_kernel(yp_ref, yc_ref, ya_ref, wp_ref, wc_ref, wa_ref,
                  gp_ref, gc_ref, ga_ref, o_ref):
    yp = jnp.dot(yp_ref[...], wp_ref[...], preferred_element_type=F32)
    yc = jnp.dot(yc_ref[...], wc_ref[...], preferred_element_type=F32)
    ya = jnp.dot(ya_ref[...], wa_ref[...], preferred_element_type=F32)
    merged = (_sigmoid(gp_ref[...]) * yp + _sigmoid(gc_ref[...]) * yc
              + _sigmoid(ga_ref[...]) * ya)
    o_ref[...] = merged.astype(BF16)


def _merge(u, yp, yc, ya, wp, wc, wa, l):
    m = u.shape[0]
    tm = min(m, 1024)
    tn = COL_BLK
    act = lambda width: pl.BlockSpec((tm, width), lambda i, j: (i, 0))
    wgt = lambda width: pl.BlockSpec((None, width, tn), lambda i, j: (l, 0, j))
    gate = lambda off: pl.BlockSpec((tm, tn), lambda i, j: (i, off + j))
    return pl.pallas_call(
        _merge_kernel,
        out_shape=jax.ShapeDtypeStruct((m, D_MODEL), BF16),
        grid=(m // tm, D_MODEL // tn),
        in_specs=[act(POOL_WIDTH), act(CONV_WIDTH), act(ATTN_WIDTH),
                  wgt(POOL_WIDTH), wgt(CONV_WIDTH), wgt(ATTN_WIDTH),
                  gate(OFF_GP), gate(OFF_GC), gate(OFF_GA)],
        out_specs=pl.BlockSpec((tm, tn), lambda i, j: (i, j)),
        compiler_params=_params(2),
        name="merge",
    )(yp, yc, ya, wp, wc, wa, u, u, u)


def _layer(x, l, pool_hist, conv_hist, hist_b0, mk, mv, kv_b0, wts, *, nb, sl, pos0):
    h = _rmsnorm(x, wts["norm_g"][l], BF16, "rmsnorm")
    u = _matmul(h, wts["w_in"], l, tm=2048, tn=COL_BLK, name="in_proj")
    yp, new_pool = _pool_branch(u, pool_hist, hist_b0, wts["pool_w"], wts["pool_scale"], l,
                                nb=nb, sl=sl, pos0=pos0)
    yc, new_conv = _conv_branch(u, conv_hist, hist_b0, wts["conv_w"], wts["conv_ln_g"],
                                wts["conv_ln_b"], l, nb=nb, sl=sl)
    ya = _attn_branch(u, mk, mv, kv_b0, nb=nb, sl=sl)
    merged = _merge(u, yp, yc, ya, wts["w_pool_out"], wts["w_conv_out"], wts["w_attn_out"], l)
    x_new = _matmul(merged, wts["w_out"], l, tm=1024, tn=COL_BLK, name="out_proj", residual=x)
    return x_new, new_pool, new_conv


def kernel(x_prompt, x_sample, cache_pool, cache_conv, cache_mem_k, cache_mem_v, mem_prompt,
           norm_g, w_in, pool_w, pool_scale, conv_w, conv_b, conv_ln_g, conv_ln_b,
           mem_norm_g, w_k, w_v, w_pool_out, w_conv_out, w_attn_out, w_out, final_g):
    bp, sp, d = x_prompt.shape
    bs, ss, _ = x_sample.shape
    depth = w_in.shape[0]

    xp = x_prompt.reshape(bp * sp, d)
    xs = x_sample.reshape(bs * ss, d)
    mem = mem_prompt.reshape(bp * N_MEM, d)

    zero_pool = jnp.zeros((bp, POOL_HIST_PAD, POOL_WIDTH), F32)
    zero_conv = jnp.zeros((bp, CONV_HIST_PAD, CONV_WIDTH), F32)
    pool_pad = jnp.pad(cache_pool, ((0, 0), (0, 0), (POOL_HIST_PAD - POOL_HIST, 0), (0, 0))
                       ).reshape(depth * bs, POOL_HIST_PAD, POOL_WIDTH)
    conv_pad = jnp.pad(cache_conv, ((0, 0), (0, 0), (CONV_HIST_PAD - CONV_HIST, 0), (0, 0))
                       ).reshape(depth * bs, CONV_HIST_PAD, CONV_WIDTH)
    cache_k = cache_mem_k.reshape(depth * bs, N_MEM, ATTN_WIDTH)
    cache_v = cache_mem_v.reshape(depth * bs, N_MEM, ATTN_WIDTH)
    conv_wb = jnp.concatenate([conv_w, conv_b[:, None, :]], axis=1)
    conv_wb = jnp.broadcast_to(conv_wb[:, :, None, :], (depth, CONV_K + 1, SUBLANES, CONV_WIDTH))

    wts = {
        "norm_g": norm_g,
        "w_in": w_in.astype(BF16),
        "pool_w": pool_w.astype(BF16),
        "pool_scale": pool_scale.reshape(depth, 1, POOL_WIDTH),
        "conv_w": conv_wb,
        "conv_ln_g": conv_ln_g.reshape(depth, 1, CONV_WIDTH),
        "conv_ln_b": conv_ln_b.reshape(depth, 1, CONV_WIDTH),
        "w_pool_out": w_pool_out.astype(BF16),
        "w_conv_out": w_conv_out.astype(BF16),
        "w_attn_out": w_attn_out.astype(BF16),
        "w_out": w_out.astype(BF16),
    }
    w_k16 = w_k.astype(BF16)
    w_v16 = w_v.astype(BF16)

    pool_p, conv_p, mk_p, mv_p, pool_s, conv_s = [], [], [], [], [], []
    for l in range(depth):
        hm = _rmsnorm(mem, mem_norm_g[l], BF16, "mem_rmsnorm")
        mk = _matmul(hm, w_k16, l, tm=1024, tn=COL_BLK, name="mem_k")
        mv = _matmul(hm, w_v16, l, tm=1024, tn=COL_BLK, name="mem_v")
        mk = mk.reshape(bp, N_MEM, ATTN_WIDTH)
        mv = mv.reshape(bp, N_MEM, ATTN_WIDTH)
        xp, np_, nc_ = _layer(xp, l, zero_pool, zero_conv, 0, mk, mv, 0, wts,
                              nb=bp, sl=sp, pos0=0)
        pool_p.append(np_)
        conv_p.append(nc_)
        mk_p.append(mk.reshape(bp, N_MEM, N_HEADS, HEAD_DIM))
        mv_p.append(mv.reshape(bp, N_MEM, N_HEADS, HEAD_DIM))

        xs, ns_, ncs_ = _layer(xs, l, pool_pad, conv_pad, l * bs, cache_k, cache_v, l * bs, wts,
                               nb=bs, sl=ss, pos0=PAST_LEN)
        pool_s.append(ns_)
        conv_s.append(ncs_)

    y_prompt = _rmsnorm(xp, final_g, F32, "final_rmsnorm").reshape(bp, sp, d)
    y_sample = _rmsnorm(xs, final_g, F32, "final_rmsnorm").reshape(bs, ss, d)
    return (y_prompt, y_sample, jnp.stack(pool_p), jnp.stack(conv_p), jnp.stack(mk_p),
            jnp.stack(mv_p), jnp.stack(pool_s), jnp.stack(conv_s))
```

```python
import functools

import jax
import jax.numpy as jnp
from jax import lax
from jax.experimental import pallas as pl
from jax.experimental.pallas import tpu as pltpu

F32 = jnp.float32
BF16 = jnp.bfloat16

D_MODEL = 4096
EPS = 1e-6
PAST_LEN = 4096
POOL_WINDOWS = (2, 4, 8, 16)
POOL_WIDTH = 1536
POOL_GROUP = 384
POOL_HIST = 15
POOL_HIST_PAD = 16
CONV_WIDTH = 1536
CONV_K = 31
CONV_HIST = 30
CONV_HIST_PAD = 32
SUBLANES = 8
ROW_CHUNK = 16
ATTN_WIDTH = 1024
N_HEADS = 4
HEAD_DIM = 256
N_MEM = 256
N_IN = 2 * POOL_WIDTH + 3 * CONV_WIDTH + 2 * ATTN_WIDTH + 3 * D_MODEL

COL_BLK = 512
OFF_PV, OFF_PZ, OFF_CA, OFF_CB, OFF_CZ, OFF_Q, OFF_AZ, OFF_GP, OFF_GC, OFF_GA = (
    0, 3, 6, 9, 12, 15, 17, 19, 27, 35)

VMEM_LIMIT_BYTES = 56 * 1024 * 1024


def _params(n_axes):
    return pltpu.CompilerParams(
        dimension_semantics=("arbitrary",) * n_axes,
        vmem_limit_bytes=VMEM_LIMIT_BYTES)


NEG_LOG2_E = -1.4426950408889634


def _sigmoid(x):
    return 1.0 / (1.0 + jnp.exp2(x * NEG_LOG2_E))


def _silu(x):
    return x * _sigmoid(x)


def _rmsnorm_kernel(x_ref, g_ref, o_ref):
    x = x_ref[...]
    ms = jnp.mean(x * x, axis=-1, keepdims=True)
    o_ref[...] = (x * lax.rsqrt(ms + EPS) * g_ref[...]).astype(o_ref.dtype)


def _rmsnorm(x, g, out_dtype, name):
    m, d = x.shape
    tr = min(m, 256)
    return pl.pallas_call(
        _rmsnorm_kernel,
        out_shape=jax.ShapeDtypeStruct((m, d), out_dtype),
        grid=(m // tr,),
        in_specs=[pl.BlockSpec((tr, d), lambda i: (i, 0)),
                  pl.BlockSpec((1, d), lambda i: (0, 0))],
        out_specs=pl.BlockSpec((tr, d), lambda i: (i, 0)),
        compiler_params=_params(1),
        name=name,
    )(x, g.reshape(1, d))


def _prenorm_kernel(x_ref, g_ref, xg_ref, r_ref):
    x = x_ref[...]
    xg_ref[...] = (x * g_ref[...]).astype(BF16)
    r_ref[...] = lax.rsqrt(jnp.mean(x * x, axis=-1, keepdims=True) + EPS)


def _prenorm(x, g, name):
    m, d = x.shape
    tr = min(m, 256)
    return pl.pallas_call(
        _prenorm_kernel,
        out_shape=(jax.ShapeDtypeStruct((m, d), BF16), jax.ShapeDtypeStruct((m, 1), F32)),
        grid=(m // tr,),
        in_specs=[pl.BlockSpec((tr, d), lambda i: (i, 0)),
                  pl.BlockSpec((1, d), lambda i: (0, 0))],
        out_specs=(pl.BlockSpec((tr, d), lambda i: (i, 0)),
                   pl.BlockSpec((tr, 1), lambda i: (i, 0))),
        compiler_params=_params(1),
        name=name,
    )(x, g.reshape(1, d))


def _mm_kernel(*refs, w_f32, emit_wb, has_scale, has_res, emit_norm, n_total):
    it = iter(refs)
    a_ref, w_ref = next(it), next(it)
    s_ref = next(it) if has_scale else None
    res_ref = next(it) if has_res else None
    g_ref = next(it) if emit_norm else None
    o_ref = next(it)
    wb_ref = next(it) if emit_wb else None
    xg_ref = next(it) if emit_norm else None
    rn_ref = next(it) if emit_norm else None

    w = w_ref[...]
    if w_f32:
        w = w.astype(BF16)
    if emit_wb:
        wb_ref[...] = w
    acc = jnp.dot(a_ref[...], w, preferred_element_type=F32)
    if has_scale:
        acc = acc * s_ref[...]
    if has_res:
        acc = res_ref[...] + acc
    o_ref[...] = acc
    if emit_norm:
        j = pl.program_id(1)
        xg_ref[...] = (acc * g_ref[...]).astype(BF16)
        ssq = jnp.sum(acc * acc, axis=-1, keepdims=True)

        @pl.when(j == 0)
        def _():
            rn_ref[...] = ssq

        @pl.when(j > 0)
        def _():
            rn_ref[...] += ssq

        @pl.when(j == pl.num_programs(1) - 1)
        def _():
            rn_ref[...] = lax.rsqrt(rn_ref[...] * (1.0 / n_total) + EPS)


def _matmul(a, w, *, tm, tn, name, layer=None, emit_wb=False, row_scale=None,
            residual=None, g_next=None):
    m, k = a.shape
    n = w.shape[-1]
    tm = min(m, tm)
    w_f32 = layer is not None
    emit_norm = g_next is not None
    assert not emit_wb or (w_f32 and m == tm)
    in_specs = [pl.BlockSpec((tm, k), lambda i, j: (i, 0))]
    if w_f32:
        in_specs.append(pl.BlockSpec((None, k, tn), lambda i, j: (layer, 0, j)))
    else:
        in_specs.append(pl.BlockSpec((k, tn), lambda i, j: (0, j)))
    args = [a, w]
    if row_scale is not None:
        in_specs.append(pl.BlockSpec((tm, 1), lambda i, j: (i, 0)))
        args.append(row_scale)
    if residual is not None:
        in_specs.append(pl.BlockSpec((tm, tn), lambda i, j: (i, j)))
        args.append(residual)
    if emit_norm:
        in_specs.append(pl.BlockSpec((1, tn), lambda i, j: (0, j)))
        args.append(g_next.reshape(1, n))
    out_shape = [jax.ShapeDtypeStruct((m, n), F32)]
    out_specs = [pl.BlockSpec((tm, tn), lambda i, j: (i, j))]
    if emit_wb:
        out_shape.append(jax.ShapeDtypeStruct((k, n), BF16))
        out_specs.append(pl.BlockSpec((k, tn), lambda i, j: (0, j)))
    if emit_norm:
        out_shape += [jax.ShapeDtypeStruct((m, n), BF16), jax.ShapeDtypeStruct((m, 1), F32)]
        out_specs += [pl.BlockSpec((tm, tn), lambda i, j: (i, j)),
                      pl.BlockSpec((tm, 1), lambda i, j: (i, 0))]
    kern = functools.partial(_mm_kernel, w_f32=w_f32, emit_wb=emit_wb,
                             has_scale=row_scale is not None, has_res=residual is not None,
                             emit_norm=emit_norm, n_total=n)
    return pl.pallas_call(
        kern,
        out_shape=tuple(out_shape),
        grid=(m // tm, n // tn),
        in_specs=in_specs,
        out_specs=tuple(out_specs),
        compiler_params=_params(2),
        name=name,
    )(*args)


def _pool_kernel(pv_ref, prev_ref, hist_ref, pz_ref, pw_ref, ps_ref,
                 y_ref, newhist_ref, xx_ref, d_ref, *, tl, rc, pos0):
    j = pl.program_id(1)

    @pl.when(j == 0)
    def _():
        xx_ref[0:POOL_HIST_PAD, :] = hist_ref[0]

    @pl.when(j > 0)
    def _():
        xx_ref[0:POOL_HIST_PAD, :] = prev_ref[...]

    xx_ref[POOL_HIST_PAD:POOL_HIST_PAD + tl, :] = pv_ref[...]

    for g, w in enumerate(POOL_WINDOWS):
        cols = slice(g * POOL_GROUP, (g + 1) * POOL_GROUP)
        for r0 in range(0, tl, rc):
            base = POOL_HIST_PAD + r0
            v = xx_ref[base:base + rc, cols]
            acc = v
            for i in range(1, w):
                acc = acc + xx_ref[base - i:base - i + rc, cols]
            t = j * tl + r0 + lax.broadcasted_iota(jnp.int32, (rc, POOL_GROUP), 0)
            cnt = jnp.minimum(w, pos0 + t + 1).astype(F32)
            d_ref[r0:r0 + rc, cols] = (acc / cnt - v).astype(BF16)

    for g in range(len(POOL_WINDOWS)):
        cols = slice(g * POOL_GROUP, (g + 1) * POOL_GROUP)
        yg = jnp.dot(d_ref[:, cols], pw_ref[g].astype(BF16), preferred_element_type=F32)
        yg = yg * ps_ref[:, cols] * _silu(pz_ref[:, cols])
        y_ref[:, cols] = yg.astype(BF16)

    @pl.when(j == pl.num_programs(1) - 1)
    def _():
        newhist_ref[0] = xx_ref[POOL_HIST_PAD + tl - POOL_HIST:POOL_HIST_PAD + tl, :]


def _pool_branch(u, hist_pad, hist_b0, pw, ps, l, *, nb, sl, pos0):
    tl = min(sl, 512)
    nj = sl // tl
    rc = min(tl, 64)
    kern = functools.partial(_pool_kernel, tl=tl, rc=rc, pos0=pos0)
    prev_blocks = tl // POOL_HIST_PAD
    n_groups = len(POOL_WINDOWS)
    return pl.pallas_call(
        kern,
        out_shape=(jax.ShapeDtypeStruct((nb * sl, POOL_WIDTH), BF16),
                   jax.ShapeDtypeStruct((nb, POOL_HIST, POOL_WIDTH), F32)),
        grid=(nb, nj),
        in_specs=[
            pl.BlockSpec((tl, POOL_WIDTH), lambda b, j: (b * nj + j, 0)),
            pl.BlockSpec((POOL_HIST_PAD, POOL_WIDTH),
                         lambda b, j: (jnp.maximum((b * nj + j) * prev_blocks - 1, 0), 0)),
            pl.BlockSpec((1, POOL_HIST_PAD, POOL_WIDTH), lambda b, j: (hist_b0 + b, 0, 0)),
            pl.BlockSpec((tl, POOL_WIDTH), lambda b, j: (b * nj + j, 1)),
            pl.BlockSpec((None, n_groups, POOL_GROUP, POOL_GROUP), lambda b, j: (l, 0, 0, 0)),
            pl.BlockSpec((None, 1, POOL_WIDTH), lambda b, j: (l, 0, 0)),
        ],
        out_specs=(pl.BlockSpec((tl, POOL_WIDTH), lambda b, j: (b * nj + j, 0)),
                   pl.BlockSpec((1, POOL_HIST, POOL_WIDTH), lambda b, j: (b, 0, 0))),
        scratch_shapes=[pltpu.VMEM((POOL_HIST_PAD + tl, POOL_WIDTH), F32),
                        pltpu.VMEM((tl, POOL_WIDTH), BF16)],
        compiler_params=_params(2),
        name="pool_branch",
    )(u, u, hist_pad, u, pw, ps)


def _conv_kernel(ca_ref, cb_ref, cz_ref, pa_ref, pb_ref, hist_ref, w_ref,
                 g_ref, beta_ref, y_ref, newhist_ref, gl_ref, yc_ref, *, tl, lw):
    j = pl.program_id(1)

    @pl.when(j == 0)
    def _():
        gl_ref[0:CONV_HIST_PAD, :] = hist_ref[0]

    @pl.when(j > 0)
    def _():
        gl_ref[0:CONV_HIST_PAD, :] = pa_ref[...] * _sigmoid(pb_ref[...])

    def glu_chunk(c, carry):
        r = pl.multiple_of(c * ROW_CHUNK, ROW_CHUNK)
        gl_ref[pl.ds(CONV_HIST_PAD + r, ROW_CHUNK), :] = (
            ca_ref[pl.ds(r, ROW_CHUNK), :] * _sigmoid(cb_ref[pl.ds(r, ROW_CHUNK), :]))
        return carry

    lax.fori_loop(0, tl // ROW_CHUNK, glu_chunk, 0)

    first_tap = CONV_HIST_PAD - CONV_HIST
    row = lax.broadcasted_iota(jnp.int32, (SUBLANES, lw), 0)
    n_tiles = tl // SUBLANES
    for lb in range(CONV_WIDTH // lw):
        lanes = slice(lb * lw, (lb + 1) * lw)

        def wrow(o, lanes=lanes):
            return w_ref[o - first_tap, :, lanes]

        def phase_sums(t0, lanes=lanes, wrow=wrow):
            tiles = [gl_ref[pl.ds(pl.multiple_of((t0 + q) * SUBLANES, SUBLANES), SUBLANES), lanes]
                     for q in range(4)]
            zs = []
            for r in range(1, SUBLANES):
                acc = None
                for q in range(4):
                    o = SUBLANES * q + r
                    if o < first_tap:
                        continue
                    term = tiles[q] * wrow(o)
                    acc = term if acc is None else acc + term
                zs.append(acc)
            return tiles, tuple(zs)

        def body(i, zcur, lanes=lanes, wrow=wrow, phase_sums=phase_sums):
            tiles, znext = phase_sums(i + 1)
            parts = [w_ref[CONV_K, :, lanes] + tiles[0] * wrow(SUBLANES)]
            for q in range(1, 4):
                parts.append(tiles[q] * wrow(SUBLANES * (q + 1)))
            for r in range(1, SUBLANES):
                m = jnp.where(row >= r, zcur[r - 1], znext[r - 1])
                parts.append(pltpu.roll(m, SUBLANES - r, axis=0))
            while len(parts) > 1:
                parts = [parts[p] + parts[p + 1] if p + 1 < len(parts) else parts[p]
                         for p in range(0, len(parts), 2)]
            yc_ref[pl.ds(pl.multiple_of(i * SUBLANES, SUBLANES), SUBLANES), lanes] = parts[0]
            return znext

        lax.fori_loop(0, n_tiles, body, phase_sums(0)[1], unroll=4)

    def ln_chunk(c, carry):
        r = pl.multiple_of(c * ROW_CHUNK, ROW_CHUNK)
        halves = []
        for half in range(ROW_CHUNK // SUBLANES):
            rows = pl.ds(pl.multiple_of(r + half * SUBLANES, SUBLANES), SUBLANES)
            y = yc_ref[rows, :]
            mu = jnp.mean(y, axis=-1, keepdims=True)
            xc = y - mu
            var = jnp.mean(xc * xc, axis=-1, keepdims=True)
            yn = xc * lax.rsqrt(var + EPS) * g_ref[...] + beta_ref[...]
            halves.append(_silu(yn) * _silu(cz_ref[rows, :]))
        y_ref[pl.ds(r, ROW_CHUNK), :] = jnp.concatenate(halves, axis=0).astype(BF16)
        return carry

    lax.fori_loop(0, tl // ROW_CHUNK, ln_chunk, 0, unroll=2)

    @pl.when(j == pl.num_programs(1) - 1)
    def _():
        newhist_ref[0] = gl_ref[CONV_HIST_PAD + tl - CONV_HIST:CONV_HIST_PAD + tl, :]


def _conv_branch(u, hist_pad, hist_b0, cw, lg, lb, l, *, nb, sl):
    tl = min(sl, 512)
    nj = sl // tl
    kern = functools.partial(_conv_kernel, tl=tl, lw=128)
    prev_blocks = tl // CONV_HIST_PAD
    row = lambda b, j: b * nj + j
    prev = lambda b, j: jnp.maximum((b * nj + j) * prev_blocks - 1, 0)
    vec = pl.BlockSpec((None, 1, CONV_WIDTH), lambda b, j: (l, 0, 0))
    return pl.pallas_call(
        kern,
        out_shape=(jax.ShapeDtypeStruct((nb * sl, CONV_WIDTH), BF16),
                   jax.ShapeDtypeStruct((nb, CONV_HIST, CONV_WIDTH), F32)),
        grid=(nb, nj),
        in_specs=[
            pl.BlockSpec((tl, CONV_WIDTH), lambda b, j: (row(b, j), 2)),
            pl.BlockSpec((tl, CONV_WIDTH), lambda b, j: (row(b, j), 3)),
            pl.BlockSpec((tl, CONV_WIDTH), lambda b, j: (row(b, j), 4)),
            pl.BlockSpec((CONV_HIST_PAD, CONV_WIDTH), lambda b, j: (prev(b, j), 2)),
            pl.BlockSpec((CONV_HIST_PAD, CONV_WIDTH), lambda b, j: (prev(b, j), 3)),
            pl.BlockSpec((1, CONV_HIST_PAD, CONV_WIDTH), lambda b, j: (hist_b0 + b, 0, 0)),
            pl.BlockSpec((None, CONV_K + 1, SUBLANES, CONV_WIDTH), lambda b, j: (l, 0, 0, 0)),
            vec, vec,
        ],
        out_specs=(pl.BlockSpec((tl, CONV_WIDTH), lambda b, j: (row(b, j), 0)),
                   pl.BlockSpec((1, CONV_HIST, CONV_WIDTH), lambda b, j: (b, 0, 0))),
        scratch_shapes=[pltpu.VMEM((CONV_HIST_PAD + tl, CONV_WIDTH), F32),
                        pltpu.VMEM((tl, CONV_WIDTH), F32)],
        compiler_params=_params(2),
        name="conv_branch",
    )(u, u, u, u, u, hist_pad, cw, lg, lb)


def _attn_kernel(q0_ref, q1_ref, z0_ref, z1_ref, k_ref, v_ref, y_ref):
    heads_per_blk = COL_BLK // HEAD_DIM
    for h in range(N_HEADS):
        q_ref = (q0_ref, q1_ref)[h // heads_per_blk]
        z_ref = (z0_ref, z1_ref)[h // heads_per_blk]
        c = slice((h % heads_per_blk) * HEAD_DIM, (h % heads_per_blk + 1) * HEAD_DIM)
        hc = slice(h * HEAD_DIM, (h + 1) * HEAD_DIM)
        q = q_ref[:, c].astype(BF16)
        k = k_ref[0, :, hc].astype(BF16)
        s = lax.dot_general(q, k, (((1,), (1,)), ((), ())), preferred_element_type=F32)
        s = s * (1.0 / 16.0)
        m = jnp.max(s, axis=-1, keepdims=True)
        e = jnp.exp(s - m)
        p = e / jnp.sum(e, axis=-1, keepdims=True)
        o = jnp.dot(p.astype(BF16), v_ref[0, :, hc].astype(BF16), preferred_element_type=F32)
        y_ref[:, hc] = (o * _silu(z_ref[:, c])).astype(BF16)


def _attn_branch(u, mk, mv, kv_b0, *, nb, sl):
    tl = min(sl, 512)
    nj = sl // tl
    ublk = lambda off: pl.BlockSpec((tl, COL_BLK), lambda b, j: (b * nj + j, off))
    kv = pl.BlockSpec((1, N_MEM, ATTN_WIDTH), lambda b, j: (kv_b0 + b, 0, 0))
    return pl.pallas_call(
        _attn_kernel,
        out_shape=jax.ShapeDtypeStruct((nb * sl, ATTN_WIDTH), BF16),
        grid=(nb, nj),
        in_specs=[ublk(OFF_Q), ublk(OFF_Q + 1), ublk(OFF_AZ), ublk(OFF_AZ + 1), kv, kv],
        out_specs=pl.BlockSpec((tl, ATTN_WIDTH), lambda b, j: (b * nj + j, 0)),
        compiler_params=_params(2),
        name="attn_branch",
    )(u, u, u, u, mk, mv)


def _merge_kernel(*refs, w_f32, emit_wb):
    (yp_ref, yc_ref, ya_ref, wp_ref, wc_ref, wa_ref, gp_ref, gc_ref, ga_ref, o_ref) = refs[:10]
    parts = []
    for k, (y_ref, w_ref, g_ref) in enumerate(((yp_ref, wp_ref, gp_ref), (yc_ref, wc_ref, gc_ref),
                                               (ya_ref, wa_ref, ga_ref))):
        w = w_ref[...]
        if w_f32:
            w = w.astype(BF16)
        if emit_wb:
            refs[10 + k][...] = w
        parts.append(_sigmoid(g_ref[...]) * jnp.dot(y_ref[...], w, preferred_element_type=F32))
    o_ref[...] = (parts[0] + parts[1] + parts[2]).astype(BF16)


def _merge(u, yp, yc, ya, wp, wc, wa, *, layer=None, emit_wb=False):
    m = u.shape[0]
    tm = min(m, 1024)
    tn = COL_BLK
    w_f32 = layer is not None
    assert not emit_wb or (w_f32 and m == tm)
    widths = (POOL_WIDTH, CONV_WIDTH, ATTN_WIDTH)
    act = lambda width: pl.BlockSpec((tm, width), lambda i, j: (i, 0))
    if w_f32:
        wgt = lambda width: pl.BlockSpec((None, width, tn), lambda i, j: (layer, 0, j))
    else:
        wgt = lambda width: pl.BlockSpec((width, tn), lambda i, j: (0, j))
    gate = lambda off: pl.BlockSpec((tm, tn), lambda i, j: (i, off + j))
    out_shape = [jax.ShapeDtypeStruct((m, D_MODEL), BF16)]
    out_specs = [pl.BlockSpec((tm, tn), lambda i, j: (i, j))]
    if emit_wb:
        out_shape += [jax.ShapeDtypeStruct((width, D_MODEL), BF16) for width in widths]
        out_specs += [pl.BlockSpec((width, tn), lambda i, j: (0, j)) for width in widths]
    return pl.pallas_call(
        functools.partial(_merge_kernel, w_f32=w_f32, emit_wb=emit_wb),
        out_shape=tuple(out_shape),
        grid=(m // tm, D_MODEL // tn),
        in_specs=[act(width) for width in widths] + [wgt(width) for width in widths]
                 + [gate(OFF_GP), gate(OFF_GC), gate(OFF_GA)],
        out_specs=tuple(out_specs),
        compiler_params=_params(2),
        name="merge",
    )(yp, yc, ya, wp, wc, wa, u, u, u)


def _layer(x, xg, r, l, pool_hist, conv_hist, hist_b0, mk, mv, kv_b0, wts, w16, g_next,
           *, nb, sl, pos0):
    convert = w16 is None
    if convert:
        u, w_in16 = _matmul(xg, wts["w_in"], tm=2048, tn=COL_BLK, name="in_proj",
                            layer=l, emit_wb=True, row_scale=r)
    else:
        (u,) = _matmul(xg, w16["w_in"], tm=2048, tn=COL_BLK, name="in_proj", row_scale=r)
    yp, new_pool = _pool_branch(u, pool_hist, hist_b0, wts["pool_w"], wts["pool_scale"], l,
                                nb=nb, sl=sl, pos0=pos0)
    yc, new_conv = _conv_branch(u, conv_hist, hist_b0, wts["conv_w"], wts["conv_ln_g"],
                                wts["conv_ln_b"], l, nb=nb, sl=sl)
    ya = _attn_branch(u, mk, mv, kv_b0, nb=nb, sl=sl)
    if convert:
        merged, wp16, wc16, wa16 = _merge(u, yp, yc, ya, wts["w_pool_out"], wts["w_conv_out"],
                                          wts["w_attn_out"], layer=l, emit_wb=True)
        outs = _matmul(merged, wts["w_out"], tm=1024, tn=COL_BLK, name="out_proj",
                       layer=l, emit_wb=True, residual=x, g_next=g_next)
        x_new, w_out16 = outs[0], outs[1]
        norm_outs = outs[2:]
        w16 = {"w_in": w_in16, "w_pool_out": wp16, "w_conv_out": wc16, "w_attn_out": wa16,
               "w_out": w_out16}
    else:
        (merged,) = _merge(u, yp, yc, ya, w16["w_pool_out"], w16["w_conv_out"], w16["w_attn_out"])
        outs = _matmul(merged, w16["w_out"], tm=1024, tn=COL_BLK, name="out_proj",
                       residual=x, g_next=g_next)
        x_new = outs[0]
        norm_outs = outs[1:]
    xg_next, r_next = norm_outs if g_next is not None else (None, None)
    return x_new, xg_next, r_next, new_pool, new_conv, w16


def kernel(x_prompt, x_sample, cache_pool, cache_conv, cache_mem_k, cache_mem_v, mem_prompt,
           norm_g, w_in, pool_w, pool_scale, conv_w, conv_b, conv_ln_g, conv_ln_b,
           mem_norm_g, w_k, w_v, w_pool_out, w_conv_out, w_attn_out, w_out, final_g):
    bp, sp, d = x_prompt.shape
    bs, ss, _ = x_sample.shape
    depth = w_in.shape[0]

    xp = x_prompt.reshape(bp * sp, d)
    xs = x_sample.reshape(bs * ss, d)
    mem = mem_prompt.reshape(bp * N_MEM, d)

    zero_pool = jnp.zeros((bp, POOL_HIST_PAD, POOL_WIDTH), F32)
    zero_conv = jnp.zeros((bp, CONV_HIST_PAD, CONV_WIDTH), F32)
    pool_pad = jnp.pad(cache_pool, ((0, 0), (0, 0), (POOL_HIST_PAD - POOL_HIST, 0), (0, 0))
                       ).reshape(depth * bs, POOL_HIST_PAD, POOL_WIDTH)
    conv_pad = jnp.pad(cache_conv, ((0, 0), (0, 0), (CONV_HIST_PAD - CONV_HIST, 0), (0, 0))
                       ).reshape(depth * bs, CONV_HIST_PAD, CONV_WIDTH)
    cache_k = cache_mem_k.reshape(depth * bs, N_MEM, ATTN_WIDTH)
    cache_v = cache_mem_v.reshape(depth * bs, N_MEM, ATTN_WIDTH)
    conv_wb = jnp.concatenate([conv_w, conv_b[:, None, :]], axis=1)
    conv_wb = jnp.broadcast_to(conv_wb[:, :, None, :], (depth, CONV_K + 1, SUBLANES, CONV_WIDTH))

    wts = {
        "w_in": w_in,
        "pool_w": pool_w,
        "pool_scale": pool_scale.reshape(depth, 1, POOL_WIDTH),
        "conv_w": conv_wb,
        "conv_ln_g": conv_ln_g.reshape(depth, 1, CONV_WIDTH),
        "conv_ln_b": conv_ln_b.reshape(depth, 1, CONV_WIDTH),
        "w_pool_out": w_pool_out,
        "w_conv_out": w_conv_out,
        "w_attn_out": w_attn_out,
        "w_out": w_out,
    }

    xgp, rp = _prenorm(xp, norm_g[0], "prenorm")
    xgs, rs = _prenorm(xs, norm_g[0], "prenorm")
    pool_p, conv_p, mk_p, mv_p, pool_s, conv_s = [], [], [], [], [], []
    for l in range(depth):
        g_next = norm_g[l + 1] if l + 1 < depth else None
        xs, xgs, rs, ns_, ncs_, w16 = _layer(
            xs, xgs, rs, l, pool_pad, conv_pad, l * bs, cache_k, cache_v, l * bs, wts, None,
            g_next, nb=bs, sl=ss, pos0=PAST_LEN)
        pool_s.append(ns_)
        conv_s.append(ncs_)

        hm = _rmsnorm(mem, mem_norm_g[l], BF16, "mem_rmsnorm")
        (mk,) = _matmul(hm, w_k, tm=1024, tn=COL_BLK, name="mem_k", layer=l)
        (mv,) = _matmul(hm, w_v, tm=1024, tn=COL_BLK, name="mem_v", layer=l)
        mk = mk.reshape(bp, N_MEM, ATTN_WIDTH)
        mv = mv.reshape(bp, N_MEM, ATTN_WIDTH)
        xp, xgp, rp, np_, nc_, _ = _layer(
            xp, xgp, rp, l, zero_pool, zero_conv, 0, mk, mv, 0, wts, w16,
            g_next, nb=bp, sl=sp, pos0=0)
        pool_p.append(np_)
        conv_p.append(nc_)
        mk_p.append(mk.reshape(bp, N_MEM, N_HEADS, HEAD_DIM))
        mv_p.append(mv.reshape(bp, N_MEM, N_HEADS, HEAD_DIM))

    y_prompt = _rmsnorm(xp, final_g, F32, "final_rmsnorm").reshape(bp, sp, d)
    y_sample = _rmsnorm(xs, final_g, F32, "final_rmsnorm").reshape(bs, ss, d)
    return (y_prompt, y_sample, jnp.stack(pool_p), jnp.stack(conv_p), jnp.stack(mk_p),
            jnp.stack(mv_p), jnp.stack(pool_s), jnp.stack(conv_s))
```

```python
import functools

import jax
import jax.numpy as jnp
from jax import lax
from jax.experimental import pallas as pl
from jax.experimental.pallas import tpu as pltpu

F32 = jnp.float32
BF16 = jnp.bfloat16

D_MODEL = 4096
EPS = 1e-6
PAST_LEN = 4096
POOL_WINDOWS = (2, 4, 8, 16)
POOL_WIDTH = 1536
POOL_GROUP = 384
POOL_HIST = 15
POOL_HIST_PAD = 16
CONV_WIDTH = 1536
CONV_K = 31
CONV_HIST = 30
CONV_HIST_PAD = 32
SUBLANES = 8
ROW_CHUNK = 16
ATTN_WIDTH = 1024
N_HEADS = 4
HEAD_DIM = 256
N_MEM = 256
N_IN = 2 * POOL_WIDTH + 3 * CONV_WIDTH + 2 * ATTN_WIDTH + 3 * D_MODEL

COL_BLK = 512
OFF_PV, OFF_PZ, OFF_CA, OFF_CB, OFF_CZ, OFF_Q, OFF_AZ, OFF_GP, OFF_GC, OFF_GA = (
    0, 3, 6, 9, 12, 15, 17, 19, 27, 35)

VMEM_LIMIT_BYTES = 56 * 1024 * 1024


def _params(n_axes):
    return pltpu.CompilerParams(
        dimension_semantics=("arbitrary",) * n_axes,
        vmem_limit_bytes=VMEM_LIMIT_BYTES)


NEG_LOG2_E = -1.4426950408889634


def _sigmoid(x):
    return 1.0 / (1.0 + jnp.exp2(x * NEG_LOG2_E))


def _silu(x):
    return x * _sigmoid(x)


def _rmsnorm_kernel(x_ref, g_ref, o_ref):
    x = x_ref[...]
    ms = jnp.mean(x * x, axis=-1, keepdims=True)
    o_ref[...] = (x * lax.rsqrt(ms + EPS) * g_ref[...]).astype(o_ref.dtype)


def _rmsnorm(x, g, out_dtype, name):
    m, d = x.shape
    tr = min(m, 256)
    return pl.pallas_call(
        _rmsnorm_kernel,
        out_shape=jax.ShapeDtypeStruct((m, d), out_dtype),
        grid=(m // tr,),
        in_specs=[pl.BlockSpec((tr, d), lambda i: (i, 0)),
                  pl.BlockSpec((1, d), lambda i: (0, 0))],
        out_specs=pl.BlockSpec((tr, d), lambda i: (i, 0)),
        compiler_params=_params(1),
        name=name,
    )(x, g.reshape(1, d))


def _prenorm_kernel(x_ref, g_ref, xg_ref, r_ref):
    x = x_ref[...]
    xg_ref[...] = (x * g_ref[...]).astype(BF16)
    r_ref[...] = lax.rsqrt(jnp.mean(x * x, axis=-1, keepdims=True) + EPS)


def _prenorm(x, g, name):
    m, d = x.shape
    tr = min(m, 256)
    return pl.pallas_call(
        _prenorm_kernel,
        out_shape=(jax.ShapeDtypeStruct((m, d), BF16), jax.ShapeDtypeStruct((m, 1), F32)),
        grid=(m // tr,),
        in_specs=[pl.BlockSpec((tr, d), lambda i: (i, 0)),
                  pl.BlockSpec((1, d), lambda i: (0, 0))],
        out_specs=(pl.BlockSpec((tr, d), lambda i: (i, 0)),
                   pl.BlockSpec((tr, 1), lambda i: (i, 0))),
        compiler_params=_params(1),
        name=name,
    )(x, g.reshape(1, d))


def _mm_kernel(*refs, w_f32, emit_wb, has_scale, has_res, emit_norm, n_total):
    it = iter(refs)
    a_ref, w_ref = next(it), next(it)
    s_ref = next(it) if has_scale else None
    res_ref = next(it) if has_res else None
    g_ref = next(it) if emit_norm else None
    o_ref = next(it)
    wb_ref = next(it) if emit_wb else None
    xg_ref = next(it) if emit_norm else None
    rn_ref = next(it) if emit_norm else None

    w = w_ref[...]
    if w_f32:
        w = w.astype(BF16)
    if emit_wb:
        wb_ref[...] = w
    acc = jnp.dot(a_ref[...], w, preferred_element_type=F32)
    if has_scale:
        acc = acc * s_ref[...]
    if has_res:
        acc = res_ref[...] + acc
    o_ref[...] = acc
    if emit_norm:
        j = pl.program_id(1)
        xg_ref[...] = (acc * g_ref[...]).astype(BF16)
        ssq = jnp.sum(acc * acc, axis=-1, keepdims=True)

        @pl.when(j == 0)
        def _():
            rn_ref[...] = ssq

        @pl.when(j > 0)
        def _():
            rn_ref[...] += ssq

        @pl.when(j == pl.num_programs(1) - 1)
        def _():
            rn_ref[...] = lax.rsqrt(rn_ref[...] * (1.0 / n_total) + EPS)


def _matmul(a, w, *, tm, tn, name, layer=None, emit_wb=False, row_scale=None,
            residual=None, g_next=None):
    m, k = a.shape
    n = w.shape[-1]
    tm = min(m, tm)
    w_f32 = layer is not None
    emit_norm = g_next is not None
    assert not emit_wb or (w_f32 and m == tm)
    in_specs = [pl.BlockSpec((tm, k), lambda i, j: (i, 0))]
    if w_f32:
        in_specs.append(pl.BlockSpec((None, k, tn), lambda i, j: (layer, 0, j)))
    else:
        in_specs.append(pl.BlockSpec((k, tn), lambda i, j: (0, j)))
    args = [a, w]
    if row_scale is not None:
        in_specs.append(pl.BlockSpec((tm, 1), lambda i, j: (i, 0)))
        args.append(row_scale)
    if residual is not None:
        in_specs.append(pl.BlockSpec((tm, tn), lambda i, j: (i, j)))
        args.append(residual)
    if emit_norm:
        in_specs.append(pl.BlockSpec((1, tn), lambda i, j: (0, j)))
        args.append(g_next.reshape(1, n))
    out_shape = [jax.ShapeDtypeStruct((m, n), F32)]
    out_specs = [pl.BlockSpec((tm, tn), lambda i, j: (i, j))]
    if emit_wb:
        out_shape.append(jax.ShapeDtypeStruct((k, n), BF16))
        out_specs.append(pl.BlockSpec((k, tn), lambda i, j: (0, j)))
    if emit_norm:
        out_shape += [jax.ShapeDtypeStruct((m, n), BF16), jax.ShapeDtypeStruct((m, 1), F32)]
        out_specs += [pl.BlockSpec((tm, tn), lambda i, j: (i, j)),
                      pl.BlockSpec((tm, 1), lambda i, j: (i, 0))]
    kern = functools.partial(_mm_kernel, w_f32=w_f32, emit_wb=emit_wb,
                             has_scale=row_scale is not None, has_res=residual is not None,
                             emit_norm=emit_norm, n_total=n)
    return pl.pallas_call(
        kern,
        out_shape=tuple(out_shape),
        grid=(m // tm, n // tn),
        in_specs=in_specs,
        out_specs=tuple(out_specs),
        compiler_params=_params(2),
        name=name,
    )(*args)


def _pool_kernel(pv_ref, prev_ref, hist_ref, pz_ref, pw_ref, ps_ref,
                 y_ref, newhist_ref, xx_ref, d_ref, *, tl, rc, pos0):
    j = pl.program_id(1)
    data0 = SUBLANES + POOL_HIST_PAD
    n_rows = data0 + tl

    xx_ref[0:SUBLANES, :] = jnp.zeros((SUBLANES, POOL_WIDTH), F32)

    @pl.when(j == 0)
    def _():
        xx_ref[SUBLANES:data0, :] = hist_ref[0]

    @pl.when(j > 0)
    def _():
        xx_ref[SUBLANES:data0, :] = prev_ref[...]

    xx_ref[data0:n_rows, :] = pv_ref[...]

    chunks = [(b, min(rc, n_rows - b)) for b in range(SUBLANES, n_rows, rc)]
    for g, w in enumerate(POOL_WINDOWS):
        cols = slice(g * POOL_GROUP, (g + 1) * POOL_GROUP)
        k = 1
        while k < w:
            for base, size in reversed(chunks):
                xx_ref[base:base + size, cols] = (xx_ref[base:base + size, cols]
                                                  + xx_ref[base - k:base - k + size, cols])
            k *= 2
        for r0 in range(0, tl, rc):
            t = j * tl + r0 + lax.broadcasted_iota(jnp.int32, (rc, 1), 0)
            inv_cnt = 1.0 / jnp.minimum(w, pos0 + t + 1).astype(F32)
            mean = xx_ref[data0 + r0:data0 + r0 + rc, cols] * inv_cnt
            d_ref[r0:r0 + rc, cols] = (mean - pv_ref[r0:r0 + rc, cols]).astype(BF16)

    for g in range(len(POOL_WINDOWS)):
        cols = slice(g * POOL_GROUP, (g + 1) * POOL_GROUP)
        yg = jnp.dot(d_ref[:, cols], pw_ref[g].astype(BF16), preferred_element_type=F32)
        yg = yg * ps_ref[:, cols] * _silu(pz_ref[:, cols])
        y_ref[:, cols] = yg.astype(BF16)

    @pl.when(j == pl.num_programs(1) - 1)
    def _():
        newhist_ref[0] = pv_ref[tl - POOL_HIST:tl, :]


def _pool_branch(u, hist_pad, hist_b0, pw, ps, l, *, nb, sl, pos0):
    tl = min(sl, 512)
    nj = sl // tl
    rc = min(tl, 64)
    assert tl >= POOL_HIST
    kern = functools.partial(_pool_kernel, tl=tl, rc=rc, pos0=pos0)
    prev_blocks = tl // POOL_HIST_PAD
    n_groups = len(POOL_WINDOWS)
    return pl.pallas_call(
        kern,
        out_shape=(jax.ShapeDtypeStruct((nb * sl, POOL_WIDTH), BF16),
                   jax.ShapeDtypeStruct((nb, POOL_HIST, POOL_WIDTH), F32)),
        grid=(nb, nj),
        in_specs=[
            pl.BlockSpec((tl, POOL_WIDTH), lambda b, j: (b * nj + j, 0)),
            pl.BlockSpec((POOL_HIST_PAD, POOL_WIDTH),
                         lambda b, j: (jnp.maximum((b * nj + j) * prev_blocks - 1, 0), 0)),
            pl.BlockSpec((1, POOL_HIST_PAD, POOL_WIDTH), lambda b, j: (hist_b0 + b, 0, 0)),
            pl.BlockSpec((tl, POOL_WIDTH), lambda b, j: (b * nj + j, 1)),
            pl.BlockSpec((None, n_groups, POOL_GROUP, POOL_GROUP), lambda b, j: (l, 0, 0, 0)),
            pl.BlockSpec((None, 1, POOL_WIDTH), lambda b, j: (l, 0, 0)),
        ],
        out_specs=(pl.BlockSpec((tl, POOL_WIDTH), lambda b, j: (b * nj + j, 0)),
                   pl.BlockSpec((1, POOL_HIST, POOL_WIDTH), lambda b, j: (b, 0, 0))),
        scratch_shapes=[pltpu.VMEM((SUBLANES + POOL_HIST_PAD + tl, POOL_WIDTH), F32),
                        pltpu.VMEM((tl, POOL_WIDTH), BF16)],
        compiler_params=_params(2),
        name="pool_branch",
    )(u, u, hist_pad, u, pw, ps)


def _conv_kernel(ca_ref, cb_ref, cz_ref, pa_ref, pb_ref, hist_ref, w_ref,
                 g_ref, beta_ref, y_ref, newhist_ref, gl_ref, yc_ref, *, tl, lw):
    j = pl.program_id(1)

    @pl.when(j == 0)
    def _():
        gl_ref[0:CONV_HIST_PAD, :] = hist_ref[0]

    @pl.when(j > 0)
    def _():
        gl_ref[0:CONV_HIST_PAD, :] = pa_ref[...] * _sigmoid(pb_ref[...])

    def glu_chunk(c, carry):
        r = pl.multiple_of(c * ROW_CHUNK, ROW_CHUNK)
        gl_ref[pl.ds(CONV_HIST_PAD + r, ROW_CHUNK), :] = (
            ca_ref[pl.ds(r, ROW_CHUNK), :] * _sigmoid(cb_ref[pl.ds(r, ROW_CHUNK), :]))
        return carry

    lax.fori_loop(0, tl // ROW_CHUNK, glu_chunk, 0)

    first_tap = CONV_HIST_PAD - CONV_HIST
    row = lax.broadcasted_iota(jnp.int32, (SUBLANES, lw), 0)
    n_tiles = tl // SUBLANES
    for lb in range(CONV_WIDTH // lw):
        lanes = slice(lb * lw, (lb + 1) * lw)

        def wrow(o, lanes=lanes):
            return w_ref[o - first_tap, :, lanes]

        def phase_sums(t0, lanes=lanes, wrow=wrow):
            tiles = [gl_ref[pl.ds(pl.multiple_of((t0 + q) * SUBLANES, SUBLANES), SUBLANES), lanes]
                     for q in range(4)]
            zs = []
            for r in range(1, SUBLANES):
                acc = None
                for q in range(4):
                    o = SUBLANES * q + r
                    if o < first_tap:
                        continue
                    term = tiles[q] * wrow(o)
                    acc = term if acc is None else acc + term
                zs.append(acc)
            return tiles, tuple(zs)

        def body(i, zcur, lanes=lanes, wrow=wrow, phase_sums=phase_sums):
            tiles, znext = phase_sums(i + 1)
            parts = [w_ref[CONV_K, :, lanes] + tiles[0] * wrow(SUBLANES)]
            for q in range(1, 4):
                parts.append(tiles[q] * wrow(SUBLANES * (q + 1)))
            for r in range(1, SUBLANES):
                m = jnp.where(row >= r, zcur[r - 1], znext[r - 1])
                parts.append(pltpu.roll(m, SUBLANES - r, axis=0))
            while len(parts) > 1:
                parts = [parts[p] + parts[p + 1] if p + 1 < len(parts) else parts[p]
                         for p in range(0, len(parts), 2)]
            yc_ref[pl.ds(pl.multiple_of(i * SUBLANES, SUBLANES), SUBLANES), lanes] = parts[0]
            return znext

        lax.fori_loop(0, n_tiles, body, phase_sums(0)[1], unroll=8)

    def ln_chunk(c, carry):
        r = pl.multiple_of(c * ROW_CHUNK, ROW_CHUNK)
        halves = []
        for half in range(ROW_CHUNK // SUBLANES):
            rows = pl.ds(pl.multiple_of(r + half * SUBLANES, SUBLANES), SUBLANES)
            y = yc_ref[rows, :]
            mu = jnp.mean(y, axis=-1, keepdims=True)
            xc = y - mu
            var = jnp.mean(xc * xc, axis=-1, keepdims=True)
            yn = xc * lax.rsqrt(var + EPS) * g_ref[...] + beta_ref[...]
            halves.append(_silu(yn) * _silu(cz_ref[rows, :]))
        y_ref[pl.ds(r, ROW_CHUNK), :] = jnp.concatenate(halves, axis=0).astype(BF16)
        return carry

    lax.fori_loop(0, tl // ROW_CHUNK, ln_chunk, 0, unroll=2)

    @pl.when(j == pl.num_programs(1) - 1)
    def _():
        newhist_ref[0] = gl_ref[CONV_HIST_PAD + tl - CONV_HIST:CONV_HIST_PAD + tl, :]


def _conv_branch(u, hist_pad, hist_b0, cw, lg, lb, l, *, nb, sl):
    tl = min(sl, 512)
    nj = sl // tl
    kern = functools.partial(_conv_kernel, tl=tl, lw=128)
    prev_blocks = tl // CONV_HIST_PAD
    row = lambda b, j: b * nj + j
    prev = lambda b, j: jnp.maximum((b * nj + j) * prev_blocks - 1, 0)
    vec = pl.BlockSpec((None, 1, CONV_WIDTH), lambda b, j: (l, 0, 0))
    return pl.pallas_call(
        kern,
        out_shape=(jax.ShapeDtypeStruct((nb * sl, CONV_WIDTH), BF16),
                   jax.ShapeDtypeStruct((nb, CONV_HIST, CONV_WIDTH), F32)),
        grid=(nb, nj),
        in_specs=[
            pl.BlockSpec((tl, CONV_WIDTH), lambda b, j: (row(b, j), 2)),
            pl.BlockSpec((tl, CONV_WIDTH), lambda b, j: (row(b, j), 3)),
            pl.BlockSpec((tl, CONV_WIDTH), lambda b, j: (row(b, j), 4)),
            pl.BlockSpec((CONV_HIST_PAD, CONV_WIDTH), lambda b, j: (prev(b, j), 2)),
            pl.BlockSpec((CONV_HIST_PAD, CONV_WIDTH), lambda b, j: (prev(b, j), 3)),
            pl.BlockSpec((1, CONV_HIST_PAD, CONV_WIDTH), lambda b, j: (hist_b0 + b, 0, 0)),
            pl.BlockSpec((None, CONV_K + 1, SUBLANES, CONV_WIDTH), lambda b, j: (l, 0, 0, 0)),
            vec, vec,
        ],
        out_specs=(pl.BlockSpec((tl, CONV_WIDTH), lambda b, j: (row(b, j), 0)),
                   pl.BlockSpec((1, CONV_HIST, CONV_WIDTH), lambda b, j: (b, 0, 0))),
        scratch_shapes=[pltpu.VMEM((CONV_HIST_PAD + tl, CONV_WIDTH), F32),
                        pltpu.VMEM((tl, CONV_WIDTH), F32)],
        compiler_params=_params(2),
        name="conv_branch",
    )(u, u, u, u, u, hist_pad, cw, lg, lb)


def _attn_kernel(q0_ref, q1_ref, z0_ref, z1_ref, k_ref, v_ref, y_ref):
    heads_per_blk = COL_BLK // HEAD_DIM
    for h in range(N_HEADS):
        q_ref = (q0_ref, q1_ref)[h // heads_per_blk]
        z_ref = (z0_ref, z1_ref)[h // heads_per_blk]
        c = slice((h % heads_per_blk) * HEAD_DIM, (h % heads_per_blk + 1) * HEAD_DIM)
        hc = slice(h * HEAD_DIM, (h + 1) * HEAD_DIM)
        q = q_ref[:, c].astype(BF16)
        k = k_ref[0, :, hc].astype(BF16)
        s = lax.dot_general(q, k, (((1,), (1,)), ((), ())), preferred_element_type=F32)
        s = s * (1.0 / 16.0)
        m = jnp.max(s, axis=-1, keepdims=True)
        e = jnp.exp(s - m)
        p = e / jnp.sum(e, axis=-1, keepdims=True)
        o = jnp.dot(p.astype(BF16), v_ref[0, :, hc].astype(BF16), preferred_element_type=F32)
        y_ref[:, hc] = (o * _silu(z_ref[:, c])).astype(BF16)


def _attn_branch(u, mk, mv, kv_b0, *, nb, sl):
    tl = min(sl, 512)
    nj = sl // tl
    ublk = lambda off: pl.BlockSpec((tl, COL_BLK), lambda b, j: (b * nj + j, off))
    kv = pl.BlockSpec((1, N_MEM, ATTN_WIDTH), lambda b, j: (kv_b0 + b, 0, 0))
    return pl.pallas_call(
        _attn_kernel,
        out_shape=jax.ShapeDtypeStruct((nb * sl, ATTN_WIDTH), BF16),
        grid=(nb, nj),
        in_specs=[ublk(OFF_Q), ublk(OFF_Q + 1), ublk(OFF_AZ), ublk(OFF_AZ + 1), kv, kv],
        out_specs=pl.BlockSpec((tl, ATTN_WIDTH), lambda b, j: (b * nj + j, 0)),
        compiler_params=_params(2),
        name="attn_branch",
    )(u, u, u, u, mk, mv)


def _merge_kernel(*refs, w_f32, emit_wb):
    (yp_ref, yc_ref, ya_ref, wp_ref, wc_ref, wa_ref, gp_ref, gc_ref, ga_ref, o_ref) = refs[:10]
    parts = []
    for k, (y_ref, w_ref, g_ref) in enumerate(((yp_ref, wp_ref, gp_ref), (yc_ref, wc_ref, gc_ref),
                                               (ya_ref, wa_ref, ga_ref))):
        w = w_ref[...]
        if w_f32:
            w = w.astype(BF16)
        if emit_wb:
            refs[10 + k][...] = w
        parts.append(_sigmoid(g_ref[...]) * jnp.dot(y_ref[...], w, preferred_element_type=F32))
    o_ref[...] = (parts[0] + parts[1] + parts[2]).astype(BF16)


def _merge(u, yp, yc, ya, wp, wc, wa, *, layer=None, emit_wb=False):
    m = u.shape[0]
    tm = min(m, 1024)
    tn = COL_BLK
    w_f32 = layer is not None
    assert not emit_wb or (w_f32 and m == tm)
    widths = (POOL_WIDTH, CONV_WIDTH, ATTN_WIDTH)
    act = lambda width: pl.BlockSpec((tm, width), lambda i, j: (i, 0))
    if w_f32:
        wgt = lambda width: pl.BlockSpec((None, width, tn), lambda i, j: (layer, 0, j))
    else:
        wgt = lambda width: pl.BlockSpec((width, tn), lambda i, j: (0, j))
    gate = lambda off: pl.BlockSpec((tm, tn), lambda i, j: (i, off + j))
    out_shape = [jax.ShapeDtypeStruct((m, D_MODEL), BF16)]
    out_specs = [pl.BlockSpec((tm, tn), lambda i, j: (i, j))]
    if emit_wb:
        out_shape += [jax.ShapeDtypeStruct((width, D_MODEL), BF16) for width in widths]
        out_specs += [pl.BlockSpec((width, tn), lambda i, j: (0, j)) for width in widths]
    return pl.pallas_call(
        functools.partial(_merge_kernel, w_f32=w_f32, emit_wb=emit_wb),
        out_shape=tuple(out_shape),
        grid=(m // tm, D_MODEL // tn),
        in_specs=[act(width) for width in widths] + [wgt(width) for width in widths]
                 + [gate(OFF_GP), gate(OFF_GC), gate(OFF_GA)],
        out_specs=tuple(out_specs),
        compiler_params=_params(2),
        name="merge",
    )(yp, yc, ya, wp, wc, wa, u, u, u)


def _layer(x, xg, r, l, pool_hist, conv_hist, hist_b0, mk, mv, kv_b0, wts, w16, g_next,
           *, nb, sl, pos0):
    convert = w16 is None
    if convert:
        u, w_in16 = _matmul(xg, wts["w_in"], tm=2048, tn=COL_BLK, name="in_proj",
                            layer=l, emit_wb=True, row_scale=r)
    else:
        (u,) = _matmul(xg, w16["w_in"], tm=2048, tn=COL_BLK, name="in_proj", row_scale=r)
    yp, new_pool = _pool_branch(u, pool_hist, hist_b0, wts["pool_w"], wts["pool_scale"], l,
                                nb=nb, sl=sl, pos0=pos0)
    yc, new_conv = _conv_branch(u, conv_hist, hist_b0, wts["conv_w"], wts["conv_ln_g"],
                                wts["conv_ln_b"], l, nb=nb, sl=sl)
    ya = _attn_branch(u, mk, mv, kv_b0, nb=nb, sl=sl)
    if convert:
        merged, wp16, wc16, wa16 = _merge(u, yp, yc, ya, wts["w_pool_out"], wts["w_conv_out"],
                                          wts["w_attn_out"], layer=l, emit_wb=True)
        outs = _matmul(merged, wts["w_out"], tm=1024, tn=COL_BLK, name="out_proj",
                       layer=l, emit_wb=True, residual=x, g_next=g_next)
        x_new, w_out16 = outs[0], outs[1]
        norm_outs = outs[2:]
        w16 = {"w_in": w_in16, "w_pool_out": wp16, "w_conv_out": wc16, "w_attn_out": wa16,
               "w_out": w_out16}
    else:
        (merged,) = _merge(u, yp, yc, ya, w16["w_pool_out"], w16["w_conv_out"], w16["w_attn_out"])
        outs = _matmul(merged, w16["w_out"], tm=1024, tn=COL_BLK, name="out_proj",
                       residual=x, g_next=g_next)
        x_new = outs[0]
        norm_outs = outs[1:]
    xg_next, r_next = norm_outs if g_next is not None else (None, None)
    return x_new, xg_next, r_next, new_pool, new_conv, w16


def kernel(x_prompt, x_sample, cache_pool, cache_conv, cache_mem_k, cache_mem_v, mem_prompt,
           norm_g, w_in, pool_w, pool_scale, conv_w, conv_b, conv_ln_g, conv_ln_b,
           mem_norm_g, w_k, w_v, w_pool_out, w_conv_out, w_attn_out, w_out, final_g):
    bp, sp, d = x_prompt.shape
    bs, ss, _ = x_sample.shape
    depth = w_in.shape[0]

    xp = x_prompt.reshape(bp * sp, d)
    xs = x_sample.reshape(bs * ss, d)
    mem = mem_prompt.reshape(bp * N_MEM, d)

    zero_pool = jnp.zeros((bp, POOL_HIST_PAD, POOL_WIDTH), F32)
    zero_conv = jnp.zeros((bp, CONV_HIST_PAD, CONV_WIDTH), F32)
    pool_pad = jnp.pad(cache_pool, ((0, 0), (0, 0), (POOL_HIST_PAD - POOL_HIST, 0), (0, 0))
                       ).reshape(depth * bs, POOL_HIST_PAD, POOL_WIDTH)
    conv_pad = jnp.pad(cache_conv, ((0, 0), (0, 0), (CONV_HIST_PAD - CONV_HIST, 0), (0, 0))
                       ).reshape(depth * bs, CONV_HIST_PAD, CONV_WIDTH)
    cache_k = cache_mem_k.reshape(depth * bs, N_MEM, ATTN_WIDTH)
    cache_v = cache_mem_v.reshape(depth * bs, N_MEM, ATTN_WIDTH)
    conv_wb = jnp.concatenate([conv_w, conv_b[:, None, :]], axis=1)
    conv_wb = jnp.broadcast_to(conv_wb[:, :, None, :], (depth, CONV_K + 1, SUBLANES, CONV_WIDTH))

    wts = {
        "w_in": w_in,
        "pool_w": pool_w,
        "pool_scale": pool_scale.reshape(depth, 1, POOL_WIDTH),
        "conv_w": conv_wb,
        "conv_ln_g": conv_ln_g.reshape(depth, 1, CONV_WIDTH),
        "conv_ln_b": conv_ln_b.reshape(depth, 1, CONV_WIDTH),
        "w_pool_out": w_pool_out,
        "w_conv_out": w_conv_out,
        "w_attn_out": w_attn_out,
        "w_out": w_out,
    }

    xgp, rp = _prenorm(xp, norm_g[0], "prenorm")
    xgs, rs = _prenorm(xs, norm_g[0], "prenorm")
    pool_p, conv_p, mk_p, mv_p, pool_s, conv_s = [], [], [], [], [], []
    for l in range(depth):
        g_next = norm_g[l + 1] if l + 1 < depth else None
        xs, xgs, rs, ns_, ncs_, w16 = _layer(
            xs, xgs, rs, l, pool_pad, conv_pad, l * bs, cache_k, cache_v, l * bs, wts, None,
            g_next, nb=bs, sl=ss, pos0=PAST_LEN)
        pool_s.append(ns_)
        conv_s.append(ncs_)

        hm = _rmsnorm(mem, mem_norm_g[l], BF16, "mem_rmsnorm")
        (mk,) = _matmul(hm, w_k, tm=1024, tn=COL_BLK, name="mem_k", layer=l)
        (mv,) = _matmul(hm, w_v, tm=1024, tn=COL_BLK, name="mem_v", layer=l)
        mk = mk.reshape(bp, N_MEM, ATTN_WIDTH)
        mv = mv.reshape(bp, N_MEM, ATTN_WIDTH)
        xp, xgp, rp, np_, nc_, _ = _layer(
            xp, xgp, rp, l, zero_pool, zero_conv, 0, mk, mv, 0, wts, w16,
            g_next, nb=bp, sl=sp, pos0=0)
        pool_p.append(np_)
        conv_p.append(nc_)
        mk_p.append(mk.reshape(bp, N_MEM, N_HEADS, HEAD_DIM))
        mv_p.append(mv.reshape(bp, N_MEM, N_HEADS, HEAD_DIM))

    y_prompt = _rmsnorm(xp, final_g, F32, "final_rmsnorm").reshape(bp, sp, d)
    y_sample = _rmsnorm(xs, final_g, F32, "final_rmsnorm").reshape(bs, ss, d)
    return (y_prompt, y_sample, jnp.stack(pool_p), jnp.stack(conv_p), jnp.stack(mk_p),
            jnp.stack(mv_p), jnp.stack(pool_s), jnp.stack(conv_s))
```

```python
import functools

import jax
import jax.numpy as jnp
from jax import lax
from jax.experimental import pallas as pl
from jax.experimental.pallas import tpu as pltpu

F32 = jnp.float32
BF16 = jnp.bfloat16

D_MODEL = 4096
EPS = 1e-6
PAST_LEN = 4096
POOL_WINDOWS = (2, 4, 8, 16)
POOL_WIDTH = 1536
POOL_GROUP = 384
POOL_HIST = 15
POOL_HIST_PAD = 16
CONV_WIDTH = 1536
CONV_K = 31
CONV_HIST = 30
CONV_HIST_PAD = 32
SUBLANES = 8
ROW_CHUNK = 16
ATTN_WIDTH = 1024
N_HEADS = 4
HEAD_DIM = 256
N_MEM = 256
N_IN = 2 * POOL_WIDTH + 3 * CONV_WIDTH + 2 * ATTN_WIDTH + 3 * D_MODEL

COL_BLK = 512
OFF_PV, OFF_PZ, OFF_CA, OFF_CB, OFF_CZ, OFF_Q, OFF_AZ, OFF_GP, OFF_GC, OFF_GA = (
    0, 3, 6, 9, 12, 15, 17, 19, 27, 35)

VMEM_LIMIT_BYTES = 56 * 1024 * 1024


def _params(n_axes):
    return pltpu.CompilerParams(
        dimension_semantics=("arbitrary",) * n_axes,
        vmem_limit_bytes=VMEM_LIMIT_BYTES)


NEG_LOG2_E = -1.4426950408889634


def _sigmoid(x):
    return 1.0 / (1.0 + jnp.exp2(x * NEG_LOG2_E))


def _silu(x):
    return x * _sigmoid(x)


def _rmsnorm_kernel(x_ref, g_ref, o_ref):
    x = x_ref[...]
    ms = jnp.mean(x * x, axis=-1, keepdims=True)
    o_ref[...] = (x * lax.rsqrt(ms + EPS) * g_ref[...]).astype(o_ref.dtype)


def _rmsnorm(x, g, out_dtype, name):
    m, d = x.shape
    tr = min(m, 256)
    return pl.pallas_call(
        _rmsnorm_kernel,
        out_shape=jax.ShapeDtypeStruct((m, d), out_dtype),
        grid=(m // tr,),
        in_specs=[pl.BlockSpec((tr, d), lambda i: (i, 0)),
                  pl.BlockSpec((1, d), lambda i: (0, 0))],
        out_specs=pl.BlockSpec((tr, d), lambda i: (i, 0)),
        compiler_params=_params(1),
        name=name,
    )(x, g.reshape(1, d))


def _prenorm_kernel(x_ref, g_ref, xg_ref, r_ref):
    x = x_ref[...]
    xg_ref[...] = (x * g_ref[...]).astype(BF16)
    r_ref[...] = lax.rsqrt(jnp.mean(x * x, axis=-1, keepdims=True) + EPS)


def _prenorm(x, g, name):
    m, d = x.shape
    tr = min(m, 256)
    return pl.pallas_call(
        _prenorm_kernel,
        out_shape=(jax.ShapeDtypeStruct((m, d), BF16), jax.ShapeDtypeStruct((m, 1), F32)),
        grid=(m // tr,),
        in_specs=[pl.BlockSpec((tr, d), lambda i: (i, 0)),
                  pl.BlockSpec((1, d), lambda i: (0, 0))],
        out_specs=(pl.BlockSpec((tr, d), lambda i: (i, 0)),
                   pl.BlockSpec((tr, 1), lambda i: (i, 0))),
        compiler_params=_params(1),
        name=name,
    )(x, g.reshape(1, d))


def _mm_kernel(*refs, w_f32, emit_wb, has_scale, has_res, emit_norm, n_total):
    it = iter(refs)
    a_ref, w_ref = next(it), next(it)
    s_ref = next(it) if has_scale else None
    res_ref = next(it) if has_res else None
    g_ref = next(it) if emit_norm else None
    o_ref = next(it)
    wb_ref = next(it) if emit_wb else None
    xg_ref = next(it) if emit_norm else None
    rn_ref = next(it) if emit_norm else None

    w = w_ref[...]
    if w_f32:
        w = w.astype(BF16)
    if emit_wb:
        wb_ref[...] = w
    acc = jnp.dot(a_ref[...], w, preferred_element_type=F32)
    if has_scale:
        acc = acc * s_ref[...]
    if has_res:
        acc = res_ref[...] + acc
    o_ref[...] = acc
    if emit_norm:
        j = pl.program_id(1)
        xg_ref[...] = (acc * g_ref[...]).astype(BF16)
        ssq = jnp.sum(acc * acc, axis=-1, keepdims=True)

        @pl.when(j == 0)
        def _():
            rn_ref[...] = ssq

        @pl.when(j > 0)
        def _():
            rn_ref[...] += ssq

        @pl.when(j == pl.num_programs(1) - 1)
        def _():
            rn_ref[...] = lax.rsqrt(rn_ref[...] * (1.0 / n_total) + EPS)


def _matmul(a, w, *, tm, tn, name, layer=None, emit_wb=False, row_scale=None,
            residual=None, g_next=None):
    m, k = a.shape
    n = w.shape[-1]
    tm = min(m, tm)
    w_f32 = layer is not None
    emit_norm = g_next is not None
    assert not emit_wb or (w_f32 and m == tm)
    in_specs = [pl.BlockSpec((tm, k), lambda i, j: (i, 0))]
    if w_f32:
        in_specs.append(pl.BlockSpec((None, k, tn), lambda i, j: (layer, 0, j)))
    else:
        in_specs.append(pl.BlockSpec((k, tn), lambda i, j: (0, j)))
    args = [a, w]
    if row_scale is not None:
        in_specs.append(pl.BlockSpec((tm, 1), lambda i, j: (i, 0)))
        args.append(row_scale)
    if residual is not None:
        in_specs.append(pl.BlockSpec((tm, tn), lambda i, j: (i, j)))
        args.append(residual)
    if emit_norm:
        in_specs.append(pl.BlockSpec((1, tn), lambda i, j: (0, j)))
        args.append(g_next.reshape(1, n))
    out_shape = [jax.ShapeDtypeStruct((m, n), F32)]
    out_specs = [pl.BlockSpec((tm, tn), lambda i, j: (i, j))]
    if emit_wb:
        out_shape.append(jax.ShapeDtypeStruct((k, n), BF16))
        out_specs.append(pl.BlockSpec((k, tn), lambda i, j: (0, j)))
    if emit_norm:
        out_shape += [jax.ShapeDtypeStruct((m, n), BF16), jax.ShapeDtypeStruct((m, 1), F32)]
        out_specs += [pl.BlockSpec((tm, tn), lambda i, j: (i, j)),
                      pl.BlockSpec((tm, 1), lambda i, j: (i, 0))]
    kern = functools.partial(_mm_kernel, w_f32=w_f32, emit_wb=emit_wb,
                             has_scale=row_scale is not None, has_res=residual is not None,
                             emit_norm=emit_norm, n_total=n)
    return pl.pallas_call(
        kern,
        out_shape=tuple(out_shape),
        grid=(m // tm, n // tn),
        in_specs=in_specs,
        out_specs=tuple(out_specs),
        compiler_params=_params(2),
        name=name,
    )(*args)


def _pool_kernel(pv_ref, prev_ref, hist_ref, pz_ref, pw_ref, ps_ref,
                 y_ref, newhist_ref, xx_ref, d_ref, *, tl, rc, pos0):
    j = pl.program_id(1)
    data0 = SUBLANES + POOL_HIST_PAD
    n_rows = data0 + tl

    xx_ref[0:SUBLANES, :] = jnp.zeros((SUBLANES, POOL_WIDTH), F32)

    @pl.when(j == 0)
    def _():
        xx_ref[SUBLANES:data0, :] = hist_ref[0]

    @pl.when(j > 0)
    def _():
        xx_ref[SUBLANES:data0, :] = prev_ref[...]

    xx_ref[data0:n_rows, :] = pv_ref[...]

    chunks = [(b, min(rc, n_rows - b)) for b in range(SUBLANES, n_rows, rc)]
    for g, w in enumerate(POOL_WINDOWS):
        cols = slice(g * POOL_GROUP, (g + 1) * POOL_GROUP)
        k = 1
        while k < w:
            for base, size in reversed(chunks):
                xx_ref[base:base + size, cols] = (xx_ref[base:base + size, cols]
                                                  + xx_ref[base - k:base - k + size, cols])
            k *= 2
        for r0 in range(0, tl, rc):
            t = j * tl + r0 + lax.broadcasted_iota(jnp.int32, (rc, 1), 0)
            inv_cnt = 1.0 / jnp.minimum(w, pos0 + t + 1).astype(F32)
            mean = xx_ref[data0 + r0:data0 + r0 + rc, cols] * inv_cnt
            d_ref[r0:r0 + rc, cols] = (mean - pv_ref[r0:r0 + rc, cols]).astype(BF16)

    for g in range(len(POOL_WINDOWS)):
        cols = slice(g * POOL_GROUP, (g + 1) * POOL_GROUP)
        yg = jnp.dot(d_ref[:, cols], pw_ref[g].astype(BF16), preferred_element_type=F32)
        yg = yg * ps_ref[:, cols] * _silu(pz_ref[:, cols])
        y_ref[:, cols] = yg.astype(BF16)

    @pl.when(j == pl.num_programs(1) - 1)
    def _():
        newhist_ref[0] = pv_ref[tl - POOL_HIST:tl, :]


def _conv_kernel(ca_ref, cb_ref, cz_ref, pa_ref, pb_ref, hist_ref, w_ref,
                 g_ref, beta_ref, y_ref, newhist_ref, gl_ref, yc_ref, *, tl, lw):
    j = pl.program_id(1)

    @pl.when(j == 0)
    def _():
        gl_ref[0:CONV_HIST_PAD, :] = hist_ref[0]

    @pl.when(j > 0)
    def _():
        gl_ref[0:CONV_HIST_PAD, :] = pa_ref[...] * _sigmoid(pb_ref[...])

    def glu_chunk(c, carry):
        r = pl.multiple_of(c * ROW_CHUNK, ROW_CHUNK)
        gl_ref[pl.ds(CONV_HIST_PAD + r, ROW_CHUNK), :] = (
            ca_ref[pl.ds(r, ROW_CHUNK), :] * _sigmoid(cb_ref[pl.ds(r, ROW_CHUNK), :]))
        return carry

    lax.fori_loop(0, tl // ROW_CHUNK, glu_chunk, 0)

    first_tap = CONV_HIST_PAD - CONV_HIST
    row = lax.broadcasted_iota(jnp.int32, (SUBLANES, lw), 0)
    n_tiles = tl // SUBLANES
    for lb in range(CONV_WIDTH // lw):
        lanes = slice(lb * lw, (lb + 1) * lw)

        def wrow(o, lanes=lanes):
            return w_ref[o - first_tap, :, lanes]

        def phase_sums(t0, lanes=lanes, wrow=wrow):
            tiles = [gl_ref[pl.ds(pl.multiple_of((t0 + q) * SUBLANES, SUBLANES), SUBLANES), lanes]
                     for q in range(4)]
            zs = []
            for r in range(1, SUBLANES):
                acc = None
                for q in range(4):
                    o = SUBLANES * q + r
                    if o < first_tap:
                        continue
                    term = tiles[q] * wrow(o)
                    acc = term if acc is None else acc + term
                zs.append(acc)
            return tiles, tuple(zs)

        def body(i, zcur, lanes=lanes, wrow=wrow, phase_sums=phase_sums):
            tiles, znext = phase_sums(i + 1)
            parts = [w_ref[CONV_K, :, lanes] + tiles[0] * wrow(SUBLANES)]
            for q in range(1, 4):
                parts.append(tiles[q] * wrow(SUBLANES * (q + 1)))
            for r in range(1, SUBLANES):
                m = jnp.where(row >= r, zcur[r - 1], znext[r - 1])
                parts.append(pltpu.roll(m, SUBLANES - r, axis=0))
            while len(parts) > 1:
                parts = [parts[p] + parts[p + 1] if p + 1 < len(parts) else parts[p]
                         for p in range(0, len(parts), 2)]
            yc_ref[pl.ds(pl.multiple_of(i * SUBLANES, SUBLANES), SUBLANES), lanes] = parts[0]
            return znext

        lax.fori_loop(0, n_tiles, body, phase_sums(0)[1], unroll=8)

    def ln_chunk(c, carry):
        r = pl.multiple_of(c * ROW_CHUNK, ROW_CHUNK)
        halves = []
        for half in range(ROW_CHUNK // SUBLANES):
            rows = pl.ds(pl.multiple_of(r + half * SUBLANES, SUBLANES), SUBLANES)
            y = yc_ref[rows, :]
            mu = jnp.mean(y, axis=-1, keepdims=True)
            xc = y - mu
            var = jnp.mean(xc * xc, axis=-1, keepdims=True)
            yn = xc * lax.rsqrt(var + EPS) * g_ref[...] + beta_ref[...]
            halves.append(_silu(yn) * _silu(cz_ref[rows, :]))
        y_ref[pl.ds(r, ROW_CHUNK), :] = jnp.concatenate(halves, axis=0).astype(BF16)
        return carry

    lax.fori_loop(0, tl // ROW_CHUNK, ln_chunk, 0, unroll=2)

    @pl.when(j == pl.num_programs(1) - 1)
    def _():
        newhist_ref[0] = gl_ref[CONV_HIST_PAD + tl - CONV_HIST:CONV_HIST_PAD + tl, :]


def _conv_branch(u, hist_pad, hist_b0, cw, lg, lb, l, *, nb, sl):
    tl = min(sl, 512)
    nj = sl // tl
    kern = functools.partial(_conv_kernel, tl=tl, lw=128)
    prev_blocks = tl // CONV_HIST_PAD
    row = lambda b, j: b * nj + j
    prev = lambda b, j: jnp.maximum((b * nj + j) * prev_blocks - 1, 0)
    vec = pl.BlockSpec((None, 1, CONV_WIDTH), lambda b, j: (l, 0, 0))
    return pl.pallas_call(
        kern,
        out_shape=(jax.ShapeDtypeStruct((nb * sl, CONV_WIDTH), BF16),
                   jax.ShapeDtypeStruct((nb, CONV_HIST, CONV_WIDTH), F32)),
        grid=(nb, nj),
        in_specs=[
            pl.BlockSpec((tl, CONV_WIDTH), lambda b, j: (row(b, j), 2)),
            pl.BlockSpec((tl, CONV_WIDTH), lambda b, j: (row(b, j), 3)),
            pl.BlockSpec((tl, CONV_WIDTH), lambda b, j: (row(b, j), 4)),
            pl.BlockSpec((CONV_HIST_PAD, CONV_WIDTH), lambda b, j: (prev(b, j), 2)),
            pl.BlockSpec((CONV_HIST_PAD, CONV_WIDTH), lambda b, j: (prev(b, j), 3)),
            pl.BlockSpec((1, CONV_HIST_PAD, CONV_WIDTH), lambda b, j: (hist_b0 + b, 0, 0)),
            pl.BlockSpec((None, CONV_K + 1, SUBLANES, CONV_WIDTH), lambda b, j: (l, 0, 0, 0)),
            vec, vec,
        ],
        out_specs=(pl.BlockSpec((tl, CONV_WIDTH), lambda b, j: (row(b, j), 0)),
                   pl.BlockSpec((1, CONV_HIST, CONV_WIDTH), lambda b, j: (b, 0, 0))),
        scratch_shapes=[pltpu.VMEM((CONV_HIST_PAD + tl, CONV_WIDTH), F32),
                        pltpu.VMEM((tl, CONV_WIDTH), F32)],
        compiler_params=_params(2),
        name="conv_branch",
    )(u, u, u, u, u, hist_pad, cw, lg, lb)


def _attn_kernel(q0_ref, q1_ref, z0_ref, z1_ref, k_ref, v_ref, y_ref):
    heads_per_blk = COL_BLK // HEAD_DIM
    for h in range(N_HEADS):
        q_ref = (q0_ref, q1_ref)[h // heads_per_blk]
        z_ref = (z0_ref, z1_ref)[h // heads_per_blk]
        c = slice((h % heads_per_blk) * HEAD_DIM, (h % heads_per_blk + 1) * HEAD_DIM)
        hc = slice(h * HEAD_DIM, (h + 1) * HEAD_DIM)
        q = q_ref[:, c].astype(BF16)
        k = k_ref[0, :, hc].astype(BF16)
        s = lax.dot_general(q, k, (((1,), (1,)), ((), ())), preferred_element_type=F32)
        s = s * (1.0 / 16.0)
        m = jnp.max(s, axis=-1, keepdims=True)
        e = jnp.exp(s - m)
        p = e / jnp.sum(e, axis=-1, keepdims=True)
        o = jnp.dot(p.astype(BF16), v_ref[0, :, hc].astype(BF16), preferred_element_type=F32)
        y_ref[:, hc] = (o * _silu(z_ref[:, c])).astype(BF16)


N_POOL_IN, N_ATTN_IN = 6, 6


def _pool_attn_kernel(*refs, tl, rc, pos0):
    pool_in = refs[:N_POOL_IN]
    attn_in = refs[N_POOL_IN:N_POOL_IN + N_ATTN_IN]
    y_pool, newhist, y_attn, xx_ref, d_ref = refs[N_POOL_IN + N_ATTN_IN:]
    _pool_kernel(*pool_in, y_pool, newhist, xx_ref, d_ref, tl=tl, rc=rc, pos0=pos0)
    _attn_kernel(*attn_in, y_attn)


def _pool_attn_branch(u, hist_pad, hist_b0, pw, ps, l, mk, mv, kv_b0, *, nb, sl, pos0):
    tl = min(sl, 512)
    nj = sl // tl
    rc = min(tl, 64)
    assert tl >= POOL_HIST
    prev_blocks = tl // POOL_HIST_PAD
    n_groups = len(POOL_WINDOWS)
    row = lambda b, j: b * nj + j
    ublk = lambda off: pl.BlockSpec((tl, COL_BLK), lambda b, j: (row(b, j), off))
    kv = pl.BlockSpec((1, N_MEM, ATTN_WIDTH), lambda b, j: (kv_b0 + b, 0, 0))
    return pl.pallas_call(
        functools.partial(_pool_attn_kernel, tl=tl, rc=rc, pos0=pos0),
        out_shape=(jax.ShapeDtypeStruct((nb * sl, POOL_WIDTH), BF16),
                   jax.ShapeDtypeStruct((nb, POOL_HIST, POOL_WIDTH), F32),
                   jax.ShapeDtypeStruct((nb * sl, ATTN_WIDTH), BF16)),
        grid=(nb, nj),
        in_specs=[
            pl.BlockSpec((tl, POOL_WIDTH), lambda b, j: (row(b, j), 0)),
            pl.BlockSpec((POOL_HIST_PAD, POOL_WIDTH),
                         lambda b, j: (jnp.maximum(row(b, j) * prev_blocks - 1, 0), 0)),
            pl.BlockSpec((1, POOL_HIST_PAD, POOL_WIDTH), lambda b, j: (hist_b0 + b, 0, 0)),
            pl.BlockSpec((tl, POOL_WIDTH), lambda b, j: (row(b, j), 1)),
            pl.BlockSpec((None, n_groups, POOL_GROUP, POOL_GROUP), lambda b, j: (l, 0, 0, 0)),
            pl.BlockSpec((None, 1, POOL_WIDTH), lambda b, j: (l, 0, 0)),
            ublk(OFF_Q), ublk(OFF_Q + 1), ublk(OFF_AZ), ublk(OFF_AZ + 1), kv, kv,
        ],
        out_specs=(pl.BlockSpec((tl, POOL_WIDTH), lambda b, j: (row(b, j), 0)),
                   pl.BlockSpec((1, POOL_HIST, POOL_WIDTH), lambda b, j: (b, 0, 0)),
                   pl.BlockSpec((tl, ATTN_WIDTH), lambda b, j: (row(b, j), 0))),
        scratch_shapes=[pltpu.VMEM((SUBLANES + POOL_HIST_PAD + tl, POOL_WIDTH), F32),
                        pltpu.VMEM((tl, POOL_WIDTH), BF16)],
        compiler_params=_params(2),
        name="pool_attn_branch",
    )(u, u, hist_pad, u, pw, ps, u, u, u, u, mk, mv)


def _merge_kernel(*refs, w_f32, emit_wb):
    (yp_ref, yc_ref, ya_ref, wp_ref, wc_ref, wa_ref, gp_ref, gc_ref, ga_ref, o_ref) = refs[:10]
    parts = []
    for k, (y_ref, w_ref, g_ref) in enumerate(((yp_ref, wp_ref, gp_ref), (yc_ref, wc_ref, gc_ref),
                                               (ya_ref, wa_ref, ga_ref))):
        w = w_ref[...]
        if w_f32:
            w = w.astype(BF16)
        if emit_wb:
            refs[10 + k][...] = w
        parts.append(_sigmoid(g_ref[...]) * jnp.dot(y_ref[...], w, preferred_element_type=F32))
    o_ref[...] = (parts[0] + parts[1] + parts[2]).astype(BF16)


def _merge(u, yp, yc, ya, wp, wc, wa, *, layer=None, emit_wb=False):
    m = u.shape[0]
    tm = min(m, 1024)
    tn = COL_BLK
    w_f32 = layer is not None
    assert not emit_wb or (w_f32 and m == tm)
    widths = (POOL_WIDTH, CONV_WIDTH, ATTN_WIDTH)
    act = lambda width: pl.BlockSpec((tm, width), lambda i, j: (i, 0))
    if w_f32:
        wgt = lambda width: pl.BlockSpec((None, width, tn), lambda i, j: (layer, 0, j))
    else:
        wgt = lambda width: pl.BlockSpec((width, tn), lambda i, j: (0, j))
    gate = lambda off: pl.BlockSpec((tm, tn), lambda i, j: (i, off + j))
    out_shape = [jax.ShapeDtypeStruct((m, D_MODEL), BF16)]
    out_specs = [pl.BlockSpec((tm, tn), lambda i, j: (i, j))]
    if emit_wb:
        out_shape += [jax.ShapeDtypeStruct((width, D_MODEL), BF16) for width in widths]
        out_specs += [pl.BlockSpec((width, tn), lambda i, j: (0, j)) for width in widths]
    return pl.pallas_call(
        functools.partial(_merge_kernel, w_f32=w_f32, emit_wb=emit_wb),
        out_shape=tuple(out_shape),
        grid=(m // tm, D_MODEL // tn),
        in_specs=[act(width) for width in widths] + [wgt(width) for width in widths]
                 + [gate(OFF_GP), gate(OFF_GC), gate(OFF_GA)],
        out_specs=tuple(out_specs),
        compiler_params=_params(2),
        name="merge",
    )(yp, yc, ya, wp, wc, wa, u, u, u)


def _layer(x, xg, r, l, pool_hist, conv_hist, hist_b0, mk, mv, kv_b0, wts, w16, g_next,
           *, nb, sl, pos0):
    convert = w16 is None
    if convert:
        u, w_in16 = _matmul(xg, wts["w_in"], tm=2048, tn=COL_BLK, name="in_proj",
                            layer=l, emit_wb=True, row_scale=r)
    else:
        (u,) = _matmul(xg, w16["w_in"], tm=2048, tn=COL_BLK, name="in_proj", row_scale=r)
    yp, new_pool, ya = _pool_attn_branch(u, pool_hist, hist_b0, wts["pool_w"], wts["pool_scale"],
                                         l, mk, mv, kv_b0, nb=nb, sl=sl, pos0=pos0)
    yc, new_conv = _conv_branch(u, conv_hist, hist_b0, wts["conv_w"], wts["conv_ln_g"],
                                wts["conv_ln_b"], l, nb=nb, sl=sl)
    if convert:
        merged, wp16, wc16, wa16 = _merge(u, yp, yc, ya, wts["w_pool_out"], wts["w_conv_out"],
                                          wts["w_attn_out"], layer=l, emit_wb=True)
        outs = _matmul(merged, wts["w_out"], tm=1024, tn=COL_BLK, name="out_proj",
                       layer=l, emit_wb=True, residual=x, g_next=g_next)
        x_new, w_out16 = outs[0], outs[1]
        norm_outs = outs[2:]
        w16 = {"w_in": w_in16, "w_pool_out": wp16, "w_conv_out": wc16, "w_attn_out": wa16,
               "w_out": w_out16}
    else:
        (merged,) = _merge(u, yp, yc, ya, w16["w_pool_out"], w16["w_conv_out"], w16["w_attn_out"])
        outs = _matmul(merged, w16["w_out"], tm=1024, tn=COL_BLK, name="out_proj",
                       residual=x, g_next=g_next)
        x_new = outs[0]
        norm_outs = outs[1:]
    xg_next, r_next = norm_outs if g_next is not None else (None, None)
    return x_new, xg_next, r_next, new_pool, new_conv, w16


def kernel(x_prompt, x_sample, cache_pool, cache_conv, cache_mem_k, cache_mem_v, mem_prompt,
           norm_g, w_in, pool_w, pool_scale, conv_w, conv_b, conv_ln_g, conv_ln_b,
           mem_norm_g, w_k, w_v, w_pool_out, w_conv_out, w_attn_out, w_out, final_g):
    bp, sp, d = x_prompt.shape
    bs, ss, _ = x_sample.shape
    depth = w_in.shape[0]

    xp = x_prompt.reshape(bp * sp, d)
    xs = x_sample.reshape(bs * ss, d)
    mem = mem_prompt.reshape(bp * N_MEM, d)

    zero_pool = jnp.zeros((bp, POOL_HIST_PAD, POOL_WIDTH), F32)
    zero_conv = jnp.zeros((bp, CONV_HIST_PAD, CONV_WIDTH), F32)
    pool_pad = jnp.pad(cache_pool, ((0, 0), (0, 0), (POOL_HIST_PAD - POOL_HIST, 0), (0, 0))
                       ).reshape(depth * bs, POOL_HIST_PAD, POOL_WIDTH)
    conv_pad = jnp.pad(cache_conv, ((0, 0), (0, 0), (CONV_HIST_PAD - CONV_HIST, 0), (0, 0))
                       ).reshape(depth * bs, CONV_HIST_PAD, CONV_WIDTH)
    cache_k = cache_mem_k.reshape(depth * bs, N_MEM, ATTN_WIDTH)
    cache_v = cache_mem_v.reshape(depth * bs, N_MEM, ATTN_WIDTH)
    conv_wb = jnp.concatenate([conv_w, conv_b[:, None, :]], axis=1)
    conv_wb = jnp.broadcast_to(conv_wb[:, :, None, :], (depth, CONV_K + 1, SUBLANES, CONV_WIDTH))

    wts = {
        "w_in": w_in,
        "pool_w": pool_w,
        "pool_scale": pool_scale.reshape(depth, 1, POOL_WIDTH),
        "conv_w": conv_wb,
        "conv_ln_g": conv_ln_g.reshape(depth, 1, CONV_WIDTH),
        "conv_ln_b": conv_ln_b.reshape(depth, 1, CONV_WIDTH),
        "w_pool_out": w_pool_out,
        "w_conv_out": w_conv_out,
        "w_attn_out": w_attn_out,
        "w_out": w_out,
    }

    xgp, rp = _prenorm(xp, norm_g[0], "prenorm")
    xgs, rs = _prenorm(xs, norm_g[0], "prenorm")
    pool_p, conv_p, mk_p, mv_p, pool_s, conv_s = [], [], [], [], [], []
    for l in range(depth):
        g_next = norm_g[l + 1] if l + 1 < depth else None
        xs, xgs, rs, ns_, ncs_, w16 = _layer(
            xs, xgs, rs, l, pool_pad, conv_pad, l * bs, cache_k, cache_v, l * bs, wts, None,
            g_next, nb=bs, sl=ss, pos0=PAST_LEN)
        pool_s.append(ns_)
        conv_s.append(ncs_)

        hm = _rmsnorm(mem, mem_norm_g[l], BF16, "mem_rmsnorm")
        (mk,) = _matmul(hm, w_k, tm=1024, tn=COL_BLK, name="mem_k", layer=l)
        (mv,) = _matmul(hm, w_v, tm=1024, tn=COL_BLK, name="mem_v", layer=l)
        mk = mk.reshape(bp, N_MEM, ATTN_WIDTH)
        mv = mv.reshape(bp, N_MEM, ATTN_WIDTH)
        xp, xgp, rp, np_, nc_, _ = _layer(
            xp, xgp, rp, l, zero_pool, zero_conv, 0, mk, mv, 0, wts, w16,
            g_next, nb=bp, sl=sp, pos0=0)
        pool_p.append(np_)
        conv_p.append(nc_)
        mk_p.append(mk.reshape(bp, N_MEM, N_HEADS, HEAD_DIM))
        mv_p.append(mv.reshape(bp, N_MEM, N_HEADS, HEAD_DIM))

    y_prompt = _rmsnorm(xp, final_g, F32, "final_rmsnorm").reshape(bp, sp, d)
    y_sample = _rmsnorm(xs, final_g, F32, "final_rmsnorm").reshape(bs, ss, d)
    return (y_prompt, y_sample, jnp.stack(pool_p), jnp.stack(conv_p), jnp.stack(mk_p),
            jnp.stack(mv_p), jnp.stack(pool_s), jnp.stack(conv_s))
```
